```python
import math
import jax
import jax.numpy as jnp
from jax import lax

D_MODEL = 4096
BATCH = 8
SEQ = 2048
DEPTH = 1

DA_HEADS = 16
DA_DH = 64
DA_DV = 2 * DA_DH
Q_BLOCK = 128
RET_HEADS = 16
RET_DK = 128
RET_DV = 256
RET_CHUNK = 128
PEER_HEADS = 8
PEER_NKEYS = 128
PEER_EXPERTS = PEER_NKEYS * PEER_NKEYS
PEER_DKEY = 256
PEER_DHALF = PEER_DKEY // 2
PEER_TOPK = 16
PEER_BLOCK = 64
NORM_EPS = 1e-5
DEEPNORM_ALPHA = (2.0 * DEPTH) ** 0.25
DEEPNORM_BETA = (8.0 * DEPTH) ** -0.25

DA_QK_W = DA_HEADS * 2 * DA_DH
DA_V_W = DA_HEADS * DA_DV
RET_QK_W = RET_HEADS * RET_DK
RET_V_W = RET_HEADS * RET_DV
IN_SPLITS = (DA_QK_W, DA_QK_W, DA_V_W, RET_QK_W, RET_QK_W, RET_V_W, RET_V_W, D_MODEL, D_MODEL)
IN_IS_VALUE = (0, 0, 1, 0, 0, 1, 0, 0, 0)
IN_COLS = sum(IN_SPLITS)

kernel_name = "hybrid_diffattn_retention_peer_deepnorm"


def _split_points(sizes):
    pts, acc = [], 0
    for s in sizes[:-1]:
        acc += s
        pts.append(acc)
    return pts


def _alibi_slopes(n_heads):
    return 2.0 ** (-8.0 * jnp.arange(1, n_heads + 1, dtype=jnp.float32) / n_heads)


def _layer_norm(x, w, b):
    xf = x.astype(jnp.float32)
    mu = jnp.mean(xf, axis=-1, keepdims=True)
    var = jnp.mean(jnp.square(xf - mu), axis=-1, keepdims=True)
    y = (xf - mu) * lax.rsqrt(var + NORM_EPS) * w.astype(jnp.float32) + b.astype(jnp.float32)
    return y.astype(x.dtype)


def diff_attention(q, k, v, lam_params, subln_w, layer_idx):
    bsz, seq = q.shape[0], q.shape[1]
    lam_init = 0.8 - 0.6 * math.exp(-0.3 * layer_idx)
    lp = lam_params.astype(jnp.float32)
    lam = jnp.exp(jnp.sum(lp[0] * lp[1])) - jnp.exp(jnp.sum(lp[2] * lp[3])) + lam_init
    qh = jnp.transpose(q.astype(jnp.float32), (0, 2, 3, 1, 4)) * (DA_DH ** -0.5)
    kh = jnp.transpose(k.astype(jnp.float32), (0, 2, 3, 1, 4))
    vh = jnp.transpose(v.astype(jnp.float32), (0, 2, 1, 3))
    slopes = _alibi_slopes(DA_HEADS)
    key_pos = jnp.arange(seq, dtype=jnp.int32)

    def one_block(blk):
        start = blk * Q_BLOCK
        qb = lax.dynamic_slice_in_dim(qh, start, Q_BLOCK, axis=3)
        logits = jnp.einsum('bhmqd,bhmkd->bhmqk', qb, kh)
        dist = (start + jnp.arange(Q_BLOCK, dtype=jnp.int32))[:, None] - key_pos[None, :]
        alibi = -slopes[:, None, None] * dist.astype(jnp.float32)
        logits = jnp.where(dist >= 0, logits + alibi[None, :, None], -jnp.inf)
        probs = jax.nn.softmax(logits, axis=-1)
        attn = probs[:, :, 0] - lam * probs[:, :, 1]
        return jnp.einsum('bhqk,bhkv->bhqv', attn, vh)

    out = lax.map(one_block, jnp.arange(seq // Q_BLOCK, dtype=jnp.int32))
    out = jnp.transpose(out, (1, 0, 3, 2, 4)).reshape(bsz, seq, DA_HEADS, DA_DV)
    out = out * lax.rsqrt(jnp.mean(jnp.square(out), axis=-1, keepdims=True) + NORM_EPS)
    out = out * subln_w.astype(jnp.float32) * (1.0 - lam_init)
    return out.reshape(bsz, seq, DA_HEADS * DA_DV).astype(q.dtype)


def retention(q, k, v):
    bsz, seq = q.shape[0], q.shape[1]
    n_chunks = seq // RET_CHUNK
    log_g = jnp.log(1.0 - 2.0 ** (-5.0 - jnp.arange(RET_HEADS, dtype=jnp.float32)))
    idx = jnp.arange(RET_CHUNK, dtype=jnp.float32)
    rel = idx[:, None] - idx[None, :]
    decay_mask = jnp.where(rel >= 0, jnp.exp(log_g[:, None, None] * jnp.maximum(rel, 0.0)), 0.0)
    cross_decay = jnp.exp(log_g[:, None] * (idx + 1.0))[None, :, :, None]
    state_decay = jnp.exp(log_g[:, None] * (RET_CHUNK - 1.0 - idx))[None, :, :, None]
    chunk_decay = jnp.exp(log_g * RET_CHUNK)[None, :, None, None]

    def to_chunks(t):
        d = t.shape[-1]
        t = t.astype(jnp.float32).reshape(bsz, n_chunks, RET_CHUNK, RET_HEADS, d)
        return jnp.transpose(t, (1, 0, 3, 2, 4))

    qc = to_chunks(q)
    kc = to_chunks(k) * (RET_DK ** -0.5)
    vc = to_chunks(v)

    def step(state, inp):
        qi, ki, vi = inp
        inner = jnp.einsum('bhid,bhjd->bhij', qi, ki) * decay_mask
        out = (jnp.einsum('bhij,bhjv->bhiv', inner, vi)
               + jnp.einsum('bhid,bhdv->bhiv', qi, state) * cross_decay)
        state = state * chunk_decay + jnp.einsum('bhjd,bhjv->bhdv', ki * state_decay, vi)
        return state, out

    state0 = jnp.zeros((bsz, RET_HEADS, RET_DK, RET_DV), jnp.float32)
    _, out = lax.scan(step, state0, (qc, kc, vc))
    return jnp.transpose(out, (1, 0, 3, 2, 4)).reshape(bsz, seq, RET_HEADS, RET_DV)


def _head_group_norm(y, w, b, out_dtype):
    bsz, seq = y.shape[0], y.shape[1]
    mu = jnp.mean(y, axis=-1, keepdims=True)
    var = jnp.mean(jnp.square(y - mu), axis=-1, keepdims=True)
    y = ((y - mu) * lax.rsqrt(var + NORM_EPS)).reshape(bsz, seq, -1)
    return (y * w.astype(jnp.float32) + b.astype(jnp.float32)).astype(out_dtype)


def peer(x, w_query, sub_keys, u_table, v_table):
    bsz, seq, dm = x.shape
    n_tok = bsz * seq
    xt = x.reshape(n_tok, dm)
    q = (xt @ w_query).reshape(n_tok, PEER_HEADS, 2, PEER_DHALF).astype(jnp.float32)
    scores = jnp.einsum('thcd,hcnd->thcn', q, sub_keys.astype(jnp.float32))
    s_top, i_top = lax.top_k(scores, PEER_TOPK)
    cand_s = s_top[:, :, 0, :, None] + s_top[:, :, 1, None, :]
    cand_i = i_top[:, :, 0, :, None] * PEER_NKEYS + i_top[:, :, 1, None, :]
    n_cand = PEER_TOPK * PEER_TOPK
    best_s, best_pos = lax.top_k(cand_s.reshape(n_tok, PEER_HEADS, n_cand), PEER_TOPK)
    best_i = jnp.take_along_axis(cand_i.reshape(n_tok, PEER_HEADS, n_cand), best_pos, axis=-1)
    gate = jax.nn.softmax(best_s, axis=-1)
    n_sel = PEER_HEADS * PEER_TOPK
    n_blk = n_tok // PEER_BLOCK
    xs = (xt.reshape(n_blk, PEER_BLOCK, dm),
          best_i.reshape(n_blk, PEER_BLOCK, n_sel),
          gate.reshape(n_blk, PEER_BLOCK, n_sel).astype(x.dtype))

    def one_block(args):
        xb, ib, gb = args
        u = jnp.take(u_table, ib, axis=0)
        h = jnp.einsum('tkd,td->tk', u, xb)
        act = jax.nn.gelu(h, approximate=False) * gb
        vv = jnp.take(v_table, ib, axis=0)
        return jnp.einsum('tk,tkd->td', act, vv)

    y = lax.map(one_block, xs)
    return y.reshape(bsz, seq, dm).astype(x.dtype)


def setup_inputs(seed: int = 0) -> dict:
    key = jax.random.key(seed)
    ks = jax.random.split(key, 18)
    f32 = jnp.float32
    beta = DEEPNORM_BETA

    def nrm(k, shape, scale):
        return jax.random.normal(k, shape, f32) * scale

    col_scale = jnp.concatenate([jnp.full((n,), beta if is_v else 1.0, f32)
                                 for n, is_v in zip(IN_SPLITS, IN_IS_VALUE)])
    return {
        "x": nrm(ks[0], (BATCH, SEQ, D_MODEL), 1.0),
        "w_in": nrm(ks[1], (DEPTH, D_MODEL, IN_COLS), D_MODEL ** -0.5) * col_scale,
        "da_lambda": nrm(ks[2], (DEPTH, 4, DA_DH), 0.1),
        "da_subln_w": 1.0 + nrm(ks[3], (DEPTH, DA_DV), 0.02),
        "w_da_out": nrm(ks[4], (DEPTH, DA_V_W, D_MODEL), beta * DA_V_W ** -0.5),
        "ret_gn_w": 1.0 + nrm(ks[5], (DEPTH, RET_V_W), 0.02),
        "ret_gn_b": nrm(ks[6], (DEPTH, RET_V_W), 0.02),
        "w_ret_out": nrm(ks[7], (DEPTH, RET_V_W, D_MODEL), beta * RET_V_W ** -0.5),
        "w_out": nrm(ks[8], (DEPTH, D_MODEL, D_MODEL), beta * D_MODEL ** -0.5),
        "ln1_w": 1.0 + nrm(ks[9], (DEPTH, D_MODEL), 0.02),
        "ln1_b": nrm(ks[10], (DEPTH, D_MODEL), 0.02),
        "peer_w_query": nrm(ks[11], (DEPTH, D_MODEL, PEER_HEADS * PEER_DKEY), D_MODEL ** -0.5),
        "peer_sub_keys": nrm(ks[12], (DEPTH, PEER_HEADS, 2, PEER_NKEYS, PEER_DHALF), PEER_DHALF ** -0.5),
        "peer_u": nrm(ks[13], (DEPTH, PEER_EXPERTS, D_MODEL), D_MODEL ** -0.5),
        "peer_v": nrm(ks[14], (DEPTH, PEER_EXPERTS, D_MODEL), beta),
        "ln2_w": 1.0 + nrm(ks[15], (DEPTH, D_MODEL), 0.02),
        "ln2_b": nrm(ks[16], (DEPTH, D_MODEL), 0.02),
    }


def reference(x, w_in, da_lambda, da_subln_w, w_da_out, ret_gn_w, ret_gn_b, w_ret_out,
              w_out, ln1_w, ln1_b, peer_w_query, peer_sub_keys, peer_u, peer_v, ln2_w, ln2_b):
    bsz, seq, _ = x.shape
    for l in range(DEPTH):
        proj = jnp.einsum('bsd,dc->bsc', x, w_in[l])
        da_q, da_k, da_v, r_q, r_k, r_v, r_g, gate_a, gate_r = jnp.split(
            proj, _split_points(IN_SPLITS), axis=-1)
        a = diff_attention(da_q.reshape(bsz, seq, DA_HEADS, 2, DA_DH),
                           da_k.reshape(bsz, seq, DA_HEADS, 2, DA_DH),
                           da_v.reshape(bsz, seq, DA_HEADS, DA_DV),
                           da_lambda[l], da_subln_w[l], l)
        a = a @ w_da_out[l]
        r = retention(r_q.reshape(bsz, seq, RET_HEADS, RET_DK),
                      r_k.reshape(bsz, seq, RET_HEADS, RET_DK),
                      r_v.reshape(bsz, seq, RET_HEADS, RET_DV))
        r = _head_group_norm(r, ret_gn_w[l], ret_gn_b[l], x.dtype) * jax.nn.silu(r_g)
        r = r @ w_ret_out[l]
        merged = jax.nn.sigmoid(gate_a) * a + jax.nn.sigmoid(gate_r) * r
        x = _layer_norm(DEEPNORM_ALPHA * x + merged @ w_out[l], ln1_w[l], ln1_b[l])
        y = peer(x, peer_w_query[l], peer_sub_keys[l], peer_u[l], peer_v[l])
        x = _layer_norm(DEEPNORM_ALPHA * x + y, ln2_w[l], ln2_b[l])
    return x
```

```python
import functools
import math

import jax
import jax.numpy as jnp
from jax import lax
from jax.experimental import pallas as pl
from jax.experimental.pallas import tpu as pltpu

DEPTH = 1
DA_HEADS = 16
DA_DH = 64
DA_DV = 2 * DA_DH
RET_HEADS = 16
RET_DK = 128
RET_DV = 256
RET_CHUNK = 128
PEER_HEADS = 8
PEER_NKEYS = 128
PEER_DHALF = 128
PEER_TOPK = 16
NORM_EPS = 1e-5
DEEPNORM_ALPHA = (2.0 * DEPTH) ** 0.25

V7X_VMEM_BYTES = 64 * 1024 * 1024
LANES = 128
F32 = jnp.float32
BF16 = jnp.bfloat16

_NT = (((1,), (1,)), ((), ()))
_TN = (((0,), (0,)), ((), ()))


def _tile(n, pref):
    if n <= pref:
        return n
    t = pref - pref % LANES
    while t >= LANES:
        if n % t == 0:
            return t
        t -= LANES
    raise ValueError(f"no tile for {n} <= {pref}")


def _params(sem, vmem_bytes):
    limit = min(int(vmem_bytes), V7X_VMEM_BYTES - 4 * 1024 * 1024)
    return pltpu.CompilerParams(dimension_semantics=sem, vmem_limit_bytes=limit)


def _sigmoid(x):
    return 1.0 / (1.0 + jnp.exp(-x))


def _mm_kernel(a_ref, b_ref, o_ref):
    o_ref[...] = jnp.dot(a_ref[...], b_ref[...], preferred_element_type=F32).astype(o_ref.dtype)


def _matmul(a, b, out_dtype, name, tm_pref=1024, tn_pref=1024):
    m, k = a.shape
    n = b.shape[1]
    tm, tn = _tile(m, tm_pref), _tile(n, tn_pref)
    osz = jnp.dtype(out_dtype).itemsize
    vmem = 2 * (tm * k * 2 + k * tn * 2 + tm * tn * osz) + tm * tn * 4 + (4 << 20)
    return pl.pallas_call(
        _mm_kernel,
        out_shape=jax.ShapeDtypeStruct((m, n), out_dtype),
        grid=(m // tm, n // tn),
        in_specs=[pl.BlockSpec((tm, k), lambda i, j: (i, 0)),
                  pl.BlockSpec((k, tn), lambda i, j: (0, j))],
        out_specs=pl.BlockSpec((tm, tn), lambda i, j: (i, j)),
        compiler_params=_params(("parallel", "parallel"), vmem),
        name=name,
    )(a, b)


def _da_kernel(slopes_ref, lam_ref, q_ref, k_ref, v_ref, w_ref, o_ref, *, tq, lam_init):
    h = pl.program_id(1)
    qi = pl.program_id(2)
    slope = slopes_ref[h]
    lp = lam_ref[...]
    lam = (jnp.exp(jnp.sum(lp[0:1] * lp[1:2], axis=-1, keepdims=True))
           - jnp.exp(jnp.sum(lp[2:3] * lp[3:4], axis=-1, keepdims=True)) + lam_init)

    q = q_ref[...]
    dh = q.shape[1] // 2
    qs = (q.astype(F32) * (dh ** -0.5)).astype(BF16)
    lane = lax.broadcasted_iota(jnp.int32, q.shape, 1)
    zero = jnp.zeros_like(qs)
    q0 = jnp.where(lane < dh, qs, zero)
    q1 = jnp.where(lane >= dh, qs, zero)
    row = lax.broadcasted_iota(jnp.int32, (tq, tq), 0)
    col = lax.broadcasted_iota(jnp.int32, (tq, tq), 1)
    rel = -slope * (row - col).astype(F32)
    dv = v_ref.shape[1]

    def update(s, v, m, l, acc):
        m_new = jnp.maximum(m, jnp.max(s, axis=-1, keepdims=True))
        alpha = jnp.exp(m - m_new)
        p = jnp.exp(s - m_new)
        l = alpha * l + jnp.sum(p, axis=-1, keepdims=True)
        acc = alpha * acc + jnp.dot(p.astype(BF16), v, preferred_element_type=F32)
        return m_new, l, acc

    def block(j, carry, diagonal):
        m0, l0, a0, m1, l1, a1 = carry
        start = pl.multiple_of(j * tq, tq)
        k = k_ref[pl.ds(start, tq), :]
        v = v_ref[pl.ds(start, tq), :]
        if diagonal:
            bias = jnp.where(col > row, -jnp.inf, rel)
        else:
            bias = rel - slope * ((qi - j) * tq).astype(F32)
        s0 = lax.dot_general(q0, k, _NT, preferred_element_type=F32) + bias
        s1 = lax.dot_general(q1, k, _NT, preferred_element_type=F32) + bias
        m0, l0, a0 = update(s0, v, m0, l0, a0)
        m1, l1, a1 = update(s1, v, m1, l1, a1)
        return m0, l0, a0, m1, l1, a1

    neg = jnp.full((tq, 1), -1e30, F32)
    zl = jnp.zeros((tq, 1), F32)
    za = jnp.zeros((tq, dv), F32)
    carry = lax.fori_loop(0, qi, lambda j, c: block(j, c, False), (neg, zl, za, neg, zl, za))
    m0, l0, a0, m1, l1, a1 = block(qi, carry, True)

    o = a0 / l0 - lam * (a1 / l1)
    ms = jnp.mean(o * o, axis=-1, keepdims=True)
    o = o * lax.rsqrt(ms + NORM_EPS) * w_ref[...] * (1.0 - lam_init)
    o_ref[...] = o.astype(o_ref.dtype)


def _diff_attention(proj, da_lambda, subln_w, bsz, seq, q_off, k_off, v_off, layer_idx):
    t = bsz * seq
    hw = 2 * DA_DH
    assert hw == DA_DV and q_off % hw == 0 and k_off % hw == 0 and v_off % DA_DV == 0
    tq = _tile(seq, 256)
    nq = seq // tq
    lam_init = 0.8 - 0.6 * math.exp(-0.3 * layer_idx)
    slopes = 2.0 ** (-8.0 * jnp.arange(1, DA_HEADS + 1, dtype=F32) / DA_HEADS)
    qb, kb, vb = q_off // hw, k_off // hw, v_off // DA_DV
    vmem = 2 * (tq * hw * 2 + 2 * seq * hw * 2 + tq * DA_DV * 2) + 16 * tq * tq * 4 + (8 << 20)
    return pl.pallas_call(
        functools.partial(_da_kernel, tq=tq, lam_init=lam_init),
        out_shape=jax.ShapeDtypeStruct((t, DA_HEADS * DA_DV), BF16),
        grid=(bsz, DA_HEADS, nq),
        in_specs=[
            pl.BlockSpec(memory_space=pltpu.SMEM),
            pl.BlockSpec((4, DA_DH), lambda b, h, i: (0, 0)),
            pl.BlockSpec((tq, hw), lambda b, h, i: (b * nq + i, qb + h)),
            pl.BlockSpec((seq, hw), lambda b, h, i: (b, kb + h)),
            pl.BlockSpec((seq, DA_DV), lambda b, h, i: (b, vb + h)),
            pl.BlockSpec((1, DA_DV), lambda b, h, i: (0, 0)),
        ],
        out_specs=pl.BlockSpec((tq, DA_DV), lambda b, h, i: (b * nq + i, h)),
        compiler_params=_params(("parallel", "parallel", "arbitrary"), vmem),
        name="diffattn",
    )(slopes, da_lambda.astype(F32), proj, proj, proj, subln_w.astype(F32).reshape(1, DA_DV))


def _ret_kernel(cd_ref, q_ref, k_ref, v_ref, g_ref, dm_ref, cross_ref, sd_ref, w_ref, b_ref,
                o_ref, state_ref, *, hg):
    grp = pl.program_id(1)
    c = pl.program_id(2)

    @pl.when(c == 0)
    def _():
        state_ref[...] = jnp.zeros_like(state_ref)

    for hh in range(hg):
        qh = q_ref[:, hh * RET_DK:(hh + 1) * RET_DK]
        kh = k_ref[:, hh * RET_DK:(hh + 1) * RET_DK]
        vh = v_ref[:, hh * RET_DV:(hh + 1) * RET_DV]
        st = state_ref[hh]
        inner = lax.dot_general(qh, kh, _NT, preferred_element_type=F32) * dm_ref[hh]
        out = (jnp.dot(inner.astype(BF16), vh, preferred_element_type=F32)
               + jnp.dot(qh, st.astype(BF16), preferred_element_type=F32) * cross_ref[hh])
        ks = (kh.astype(F32) * sd_ref[hh]).astype(BF16)
        state_ref[hh] = (st * cd_ref[grp * hg + hh]
                         + lax.dot_general(ks, vh, _TN, preferred_element_type=F32))
        mu = jnp.mean(out, axis=-1, keepdims=True)
        d = out - mu
        var = jnp.mean(d * d, axis=-1, keepdims=True)
        sl = slice(hh * RET_DV, (hh + 1) * RET_DV)
        y = d * lax.rsqrt(var + NORM_EPS) * w_ref[:, sl] + b_ref[:, sl]
        g = g_ref[:, sl].astype(F32)
        o_ref[:, sl] = (y * (g * _sigmoid(g))).astype(o_ref.dtype)


def _retention(proj, gn_w, gn_b, bsz, seq, q_off, k_off, v_off, g_off):
    t = bsz * seq
    nc = seq // RET_CHUNK
    hg = min(8, RET_HEADS)
    ngrp = RET_HEADS // hg
    qw, vw = hg * RET_DK, hg * RET_DV
    assert q_off % qw == 0 and k_off % qw == 0 and v_off % vw == 0 and g_off % vw == 0
    log_g = jnp.log(1.0 - 2.0 ** (-5.0 - jnp.arange(RET_HEADS, dtype=F32)))
    idx = jnp.arange(RET_CHUNK, dtype=F32)
    rel = idx[:, None] - idx[None, :]
    scale = RET_DK ** -0.5
    dmask = jnp.where(rel >= 0, jnp.exp(log_g[:, None, None] * jnp.maximum(rel, 0.0)), 0.0) * scale
    cross = jnp.exp(log_g[:, None] * (idx + 1.0))[:, :, None]
    sdec = (jnp.exp(log_g[:, None] * (RET_CHUNK - 1.0 - idx)) * scale)[:, :, None]
    cdec = jnp.exp(log_g * RET_CHUNK)
    qb, kb, vb, gb = q_off // qw, k_off // qw, v_off // vw, g_off // vw
    c_ = RET_CHUNK
    vmem = (2 * (2 * c_ * qw * 2 + 3 * c_ * vw * 2 + hg * c_ * c_ * 4 + 2 * hg * c_ * LANES * 4)
            + hg * RET_DK * RET_DV * 4 + (16 << 20))
    return pl.pallas_call(
        functools.partial(_ret_kernel, hg=hg),
        out_shape=jax.ShapeDtypeStruct((t, RET_HEADS * RET_DV), BF16),
        grid=(bsz, ngrp, nc),
        in_specs=[
            pl.BlockSpec(memory_space=pltpu.SMEM),
            pl.BlockSpec((c_, qw), lambda b, g, c: (b * nc + c, qb + g)),
            pl.BlockSpec((c_, qw), lambda b, g, c: (b * nc + c, kb + g)),
            pl.BlockSpec((c_, vw), lambda b, g, c: (b * nc + c, vb + g)),
            pl.BlockSpec((c_, vw), lambda b, g, c: (b * nc + c, gb + g)),
            pl.BlockSpec((hg, c_, c_), lambda b, g, c: (g, 0, 0)),
            pl.BlockSpec((hg, c_, 1), lambda b, g, c: (g, 0, 0)),
            pl.BlockSpec((hg, c_, 1), lambda b, g, c: (g, 0, 0)),
            pl.BlockSpec((1, vw), lambda b, g, c: (0, g)),
            pl.BlockSpec((1, vw), lambda b, g, c: (0, g)),
        ],
        out_specs=pl.BlockSpec((c_, vw), lambda b, g, c: (b * nc + c, g)),
        scratch_shapes=[pltpu.VMEM((hg, RET_DK, RET_DV), F32)],
        compiler_params=_params(("parallel", "parallel", "arbitrary"), vmem),
        name="retention",
    )(cdec, proj, proj, proj, proj, dmask, cross, sdec,
      gn_w.astype(F32).reshape(1, -1), gn_b.astype(F32).reshape(1, -1))


def _merge_kernel(a_ref, wd_ref, r_ref, wr_ref, ga_ref, gr_ref, o_ref):
    a2 = jnp.dot(a_ref[...], wd_ref[...], preferred_element_type=F32)
    r2 = jnp.dot(r_ref[...], wr_ref[...], preferred_element_type=F32)
    ga = ga_ref[...].astype(F32)
    gr = gr_ref[...].astype(F32)
    o_ref[...] = (_sigmoid(ga) * a2 + _sigmoid(gr) * r2).astype(o_ref.dtype)


def _merge(a, wd, r, wr, proj, ga_off, gr_off, d_model):
    t = a.shape[0]
    ka, kr = a.shape[1], r.shape[1]
    tm, tn = _tile(t, 512), _tile(d_model, 1024)
    assert ga_off % tn == 0 and gr_off % tn == 0
    gab, grb = ga_off // tn, gr_off // tn
    vmem = 2 * ((tm + tn) * (ka + kr) * 2 + 3 * tm * tn * 2) + 3 * tm * tn * 4 + (4 << 20)
    return pl.pallas_call(
        _merge_kernel,
        out_shape=jax.ShapeDtypeStruct((t, d_model), BF16),
        grid=(t // tm, d_model // tn),
        in_specs=[
            pl.BlockSpec((tm, ka), lambda i, j: (i, 0)),
            pl.BlockSpec((ka, tn), lambda i, j: (0, j)),
            pl.BlockSpec((tm, kr), lambda i, j: (i, 0)),
            pl.BlockSpec((kr, tn), lambda i, j: (0, j)),
            pl.BlockSpec((tm, tn), lambda i, j: (i, gab + j)),
            pl.BlockSpec((tm, tn), lambda i, j: (i, grb + j)),
        ],
        out_specs=pl.BlockSpec((tm, tn), lambda i, j: (i, j)),
        compiler_params=_params(("parallel", "parallel"), vmem),
        name="merge",
    )(a, wd, r, wr, proj, proj)


def _layer_norm_slabs(acc_ref, nj, tn, lw_ref, lb_ref, write):
    d = nj * tn
    tot = acc_ref[0].sum(axis=-1, keepdims=True)
    for jj in range(1, nj):
        tot = tot + acc_ref[jj].sum(axis=-1, keepdims=True)
    mu = tot / d
    sq = jnp.square(acc_ref[0] - mu).sum(axis=-1, keepdims=True)
    for jj in range(1, nj):
        sq = sq + jnp.square(acc_ref[jj] - mu).sum(axis=-1, keepdims=True)
    rstd = lax.rsqrt(sq / d + NORM_EPS)
    for jj in range(nj):
        sl = slice(jj * tn, (jj + 1) * tn)
        write(sl, (acc_ref[jj] - mu) * rstd * lw_ref[:, sl] + lb_ref[:, sl])


def _outln_kernel(m_ref, w_ref, x_ref, lw_ref, lb_ref, of_ref, ob_ref, acc_ref, *, nj, tn):
    j = pl.program_id(1)
    acc_ref[j] = DEEPNORM_ALPHA * x_ref[...] + jnp.dot(m_ref[...], w_ref[...],
                                                        preferred_element_type=F32)

    @pl.when(j == nj - 1)
    def _():
        def write(sl, y):
            of_ref[:, sl] = y
            ob_ref[:, sl] = y.astype(ob_ref.dtype)
        _layer_norm_slabs(acc_ref, nj, tn, lw_ref, lb_ref, write)


def _out_ln(merged, w_out, x2d, ln_w, ln_b):
    t, d = x2d.shape
    tm, tn = _tile(t, 256), _tile(d, 1024)
    nj = d // tn
    vmem = (2 * (tm * d * 2 + d * tn * 2 + tm * tn * 4 + tm * d * 4 + tm * d * 2)
            + tm * d * 4 + 3 * tm * tn * 4 + (4 << 20))
    return pl.pallas_call(
        functools.partial(_outln_kernel, nj=nj, tn=tn),
        out_shape=(jax.ShapeDtypeStruct((t, d), F32), jax.ShapeDtypeStruct((t, d), BF16)),
        grid=(t // tm, nj),
        in_specs=[
            pl.BlockSpec((tm, d), lambda i, j: (i, 0)),
            pl.BlockSpec((d, tn), lambda i, j: (0, j)),
            pl.BlockSpec((tm, tn), lambda i, j: (i, j)),
            pl.BlockSpec((1, d), lambda i, j: (0, 0)),
            pl.BlockSpec((1, d), lambda i, j: (0, 0)),
        ],
        out_specs=(pl.BlockSpec((tm, d), lambda i, j: (i, 0)),
                   pl.BlockSpec((tm, d), lambda i, j: (i, 0))),
        scratch_shapes=[pltpu.VMEM((nj, tm, tn), F32)],
        compiler_params=_params(("parallel", "arbitrary"), vmem),
        name="outln",
    )(merged, w_out, x2d, ln_w.astype(F32).reshape(1, d), ln_b.astype(F32).reshape(1, d))


_NOT_TOP = 99.0


def _top16(s):
    nk, tl = s.shape
    key = lax.broadcasted_iota(jnp.int32, (nk, tl), 0).astype(F32)
    rank = jnp.full((nk, tl), _NOT_TOP, F32)
    vals = []
    for r in range(PEER_TOPK):
        m = jnp.max(s, axis=0, keepdims=True)
        first = jnp.min(jnp.where(s == m, key, float(nk)), axis=0, keepdims=True)
        hit = key == first
        rank = jnp.where(hit, float(r), rank)
        s = jnp.where(hit, -jnp.inf, s)
        vals.append(m)
    return jnp.concatenate(vals, axis=0), rank


def _route_kernel(x_ref, wq_ref, keys_ref, r2_ref, e2_ref, n_ref, c_ref, q_scr):
    h = pl.program_id(1)
    ngrp = q_scr.shape[0]
    dk = q_scr.shape[2]

    @pl.when(h == 0)
    def _():
        q = jnp.dot(x_ref[...], wq_ref[...], preferred_element_type=F32)
        for g in range(ngrp):
            q_scr[g] = q[:, g * dk:(g + 1) * dk].astype(q_scr.dtype)

    s1 = lax.dot_general(keys_ref[2 * h], q_scr[2 * h], _NT, preferred_element_type=F32)
    s2 = lax.dot_general(keys_ref[2 * h + 1], q_scr[2 * h + 1], _NT, preferred_element_type=F32)
    a, rank1 = _top16(s1)
    b, rank2 = _top16(s2)
    tl = s1.shape[1]
    k = PEER_TOPK
    ea = jnp.exp(a - a[0:1])
    eb = jnp.exp(b - b[0:1])

    half = k // 2
    i_k = lax.broadcasted_iota(jnp.int32, (k, tl), 0).astype(F32)
    i_h = lax.broadcasted_iota(jnp.int32, (half, tl), 0).astype(F32)
    cand = [a[0:1] + b]
    prod = [ea[0:1] * eb]
    pos = [i_k]
    for r1 in range(1, half):
        cand.append(a[r1:r1 + 1] + b[0:half])
        prod.append(ea[r1:r1 + 1] * eb[0:half])
        pos.append(i_h + float(r1 * k))
    cand.append(a[half:k] + b[0:1])
    prod.append(ea[half:k] * eb[0:1])
    pos.append((i_h + float(half)) * float(k))
    cand = jnp.concatenate(cand, axis=0)
    prod = jnp.concatenate(prod, axis=0)
    pos = jnp.concatenate(pos, axis=0)
    picked = jnp.zeros_like(cand)
    for _ in range(k):
        m = jnp.max(cand, axis=0, keepdims=True)
        first = jnp.min(jnp.where(cand == m, pos, float(k * k)), axis=0, keepdims=True)
        hit = pos == first
        picked = jnp.where(hit, 1.0, picked)
        cand = jnp.where(hit, -jnp.inf, cand)
    z = jnp.sum(picked * prod, axis=0, keepdims=True)

    counts = [jnp.sum(picked[0:k], axis=0, keepdims=True)]
    for r1 in range(1, half):
        lo = k + (r1 - 1) * half
        counts.append(jnp.sum(picked[lo:lo + half], axis=0, keepdims=True))
    lo = k + (half - 1) * half
    for r in range(half):
        counts.append(picked[lo + r:lo + r + 1])
    n = jnp.zeros_like(s1)
    for r1 in range(k):
        n = jnp.where(rank1 == float(r1), counts[r1], n)

    r2_ref[0] = rank2
    e2_ref[0] = jnp.exp(s2 - b[0:1])
    n_ref[0] = n
    c_ref[0] = jnp.exp(s1 - a[0:1]) / z


def _route(x1b, wq, keys):
    t, d = x1b.shape
    nq = wq.shape[1]
    ngrp = 2 * PEER_HEADS
    assert nq == ngrp * PEER_DHALF and PEER_TOPK == 16 and PEER_NKEYS % 8 == 0
    tl = _tile(t, 256)
    shp = jax.ShapeDtypeStruct((PEER_HEADS, PEER_NKEYS, t), F32)
    ospec = pl.BlockSpec((1, PEER_NKEYS, tl), lambda i, h: (h, 0, i))
    vmem = (2 * tl * d * 2 + d * nq * 2 + 2 * ngrp * PEER_NKEYS * PEER_DHALF * 2
            + 8 * PEER_NKEYS * tl * 4 + ngrp * tl * PEER_DHALF * 2 + tl * nq * 4 + (16 << 20))
    return pl.pallas_call(
        _route_kernel,
        out_shape=(shp, shp, shp, shp),
        grid=(t // tl, PEER_HEADS),
        in_specs=[
            pl.BlockSpec((tl, d), lambda i, h: (i, 0)),
            pl.BlockSpec((d, nq), lambda i, h: (0, 0), pipeline_mode=pl.Buffered(1)),
            pl.BlockSpec((ngrp, PEER_NKEYS, PEER_DHALF), lambda i, h: (0, 0, 0)),
        ],
        out_specs=(ospec, ospec, ospec, ospec),
        scratch_shapes=[pltpu.VMEM((ngrp, tl, PEER_DHALF), BF16)],
        compiler_params=_params(("parallel", "arbitrary"), vmem),
        name="route",
    )(x1b, wq, keys)


def _peer_kernel(x_ref, u_ref, v_ref, r2_ref, e2_ref, n_ref, c_ref, y_ref, *, rows):
    j = pl.program_id(1)

    @pl.when(j == 0)
    def _():
        y_ref[...] = jnp.zeros_like(y_ref)

    ht = lax.dot_general(u_ref[...], x_ref[...], _NT, preferred_element_type=F32)
    nk = PEER_NKEYS
    acts = []
    for a in range(rows):
        i1 = j * rows + a
        g = None
        for h in range(PEER_HEADS):
            nrow = n_ref[h, pl.ds(i1, 1), :]
            crow = c_ref[h, pl.ds(i1, 1), :]
            term = jnp.where(r2_ref[h] < nrow, e2_ref[h] * crow, 0.0)
            g = term if g is None else g + term
        hh = ht[a * nk:(a + 1) * nk]
        gelu = hh * (lax.erf(hh * (2.0 ** -0.5)) + 1.0) * 0.5
        acts.append((gelu * g).astype(BF16))
    act_t = jnp.concatenate(acts, axis=0)
    y_ref[...] += lax.dot_general(act_t, v_ref[...], _TN, preferred_element_type=F32)


def _peer_dense(x1b, u, v, r2, e2, nn, cc):
    t, d = x1b.shape
    ne = u.shape[0]
    assert ne == PEER_NKEYS * PEER_NKEYS
    tm = _tile(t, 512)
    rows = 4
    te = rows * PEER_NKEYS
    rspec = pl.BlockSpec((PEER_HEADS, PEER_NKEYS, tm), lambda i, j: (0, 0, i),
                         pipeline_mode=pl.Buffered(1))
    vmem = (tm * d * 2 + 4 * te * d * 2 + 4 * PEER_HEADS * PEER_NKEYS * tm * 4 + 2 * tm * d * 4
            + 4 * te * tm * 4 + (4 << 20))
    return pl.pallas_call(
        functools.partial(_peer_kernel, rows=rows),
        out_shape=jax.ShapeDtypeStruct((t, d), F32),
        grid=(t // tm, ne // te),
        in_specs=[
            pl.BlockSpec((tm, d), lambda i, j: (i, 0), pipeline_mode=pl.Buffered(1)),
            pl.BlockSpec((te, d), lambda i, j: (j, 0)),
            pl.BlockSpec((te, d), lambda i, j: (j, 0)),
            rspec, rspec, rspec, rspec,
        ],
        out_specs=pl.BlockSpec((tm, d), lambda i, j: (i, 0)),
        compiler_params=_params(("parallel", "arbitrary"), vmem),
        name="peer",
    )(x1b, u, v, r2, e2, nn, cc)


def _ln2_kernel(x_ref, y_ref, w_ref, b_ref, o_ref):
    z = DEEPNORM_ALPHA * x_ref[...] + y_ref[...]
    mu = jnp.mean(z, axis=-1, keepdims=True)
    d = z - mu
    var = jnp.mean(d * d, axis=-1, keepdims=True)
    o_ref[...] = (d * lax.rsqrt(var + NORM_EPS) * w_ref[...] + b_ref[...]).astype(o_ref.dtype)


def _ln2(x1f, y, ln_w, ln_b):
    t, d = x1f.shape
    tm = _tile(t, 256)
    vmem = 2 * 3 * tm * d * 4 + 3 * tm * d * 4 + (4 << 20)
    return pl.pallas_call(
        _ln2_kernel,
        out_shape=jax.ShapeDtypeStruct((t, d), F32),
        grid=(t // tm,),
        in_specs=[
            pl.BlockSpec((tm, d), lambda i: (i, 0)),
            pl.BlockSpec((tm, d), lambda i: (i, 0)),
            pl.BlockSpec((1, d), lambda i: (0, 0)),
            pl.BlockSpec((1, d), lambda i: (0, 0)),
        ],
        out_specs=pl.BlockSpec((tm, d), lambda i: (i, 0)),
        compiler_params=_params(("parallel",), vmem),
        name="ln2",
    )(x1f, y, ln_w.astype(F32).reshape(1, d), ln_b.astype(F32).reshape(1, d))


def kernel(x, w_in, da_lambda, da_subln_w, w_da_out, ret_gn_w, ret_gn_b, w_ret_out, w_out,
           ln1_w, ln1_b, peer_w_query, peer_sub_keys, peer_u, peer_v, ln2_w, ln2_b):
    bsz, seq, d = x.shape
    t = bsz * seq
    da_qk_w, da_v_w = DA_HEADS * 2 * DA_DH, DA_HEADS * DA_DV
    ret_qk_w, ret_v_w = RET_HEADS * RET_DK, RET_HEADS * RET_DV
    splits = (da_qk_w, da_qk_w, da_v_w, ret_qk_w, ret_qk_w, ret_v_w, ret_v_w, d, d)
    offs = [0]
    for s in splits:
        offs.append(offs[-1] + s)
    o_q, o_k, o_v, o_rq, o_rk, o_rv, o_rg, o_ga, o_gr = offs[:-1]

    xf = x.reshape(t, d)
    for l in range(DEPTH):
        xb = xf.astype(BF16)
        proj = _matmul(xb, w_in[l].astype(BF16), BF16, "proj")
        a = _diff_attention(proj, da_lambda[l], da_subln_w[l], bsz, seq, o_q, o_k, o_v, l)
        r = _retention(proj, ret_gn_w[l], ret_gn_b[l], bsz, seq, o_rq, o_rk, o_rv, o_rg)
        merged = _merge(a, w_da_out[l].astype(BF16), r, w_ret_out[l].astype(BF16), proj, o_ga, o_gr, d)
        x1f, x1b = _out_ln(merged, w_out[l].astype(BF16), xf, ln1_w[l], ln1_b[l])
        keys = peer_sub_keys[l].astype(BF16).reshape(2 * PEER_HEADS, PEER_NKEYS, PEER_DHALF)
        r2, e2, nn, cc = _route(x1b, peer_w_query[l].astype(BF16), keys)
        y = _peer_dense(x1b, peer_u[l].astype(BF16), peer_v[l].astype(BF16), r2, e2, nn, cc)
        xf = _ln2(x1f, y, ln2_w[l], ln2_b[l])
    return xf.reshape(bsz, seq, d)
```

```python
import functools
import math

import jax
import jax.numpy as jnp
from jax import lax
from jax.experimental import pallas as pl
from jax.experimental.pallas import tpu as pltpu

DEPTH = 1
DA_HEADS = 16
DA_DH = 64
DA_DV = 2 * DA_DH
RET_HEADS = 16
RET_DK = 128
RET_DV = 256
RET_CHUNK = 128
PEER_HEADS = 8
PEER_NKEYS = 128
PEER_DHALF = 128
PEER_TOPK = 16
NORM_EPS = 1e-5
DEEPNORM_ALPHA = (2.0 * DEPTH) ** 0.25

V7X_VMEM_BYTES = 64 * 1024 * 1024
LANES = 128
SOFTMAX_ROWS = 128
F32 = jnp.float32
BF16 = jnp.bfloat16

_NT = (((1,), (1,)), ((), ()))
_TN = (((0,), (0,)), ((), ()))


def _tile(n, pref):
    if n <= pref:
        return n
    t = pref - pref % LANES
    while t >= LANES:
        if n % t == 0:
            return t
        t -= LANES
    raise ValueError(f"no tile for {n} <= {pref}")


def _params(sem, vmem_bytes, flags=None):
    limit = min(int(vmem_bytes), V7X_VMEM_BYTES - 4 * 1024 * 1024)
    return pltpu.CompilerParams(dimension_semantics=sem, vmem_limit_bytes=limit, flags=flags)


def _sigmoid(x):
    return 1.0 / (1.0 + jnp.exp(-x))


def _mm_kernel(a_ref, b_ref, o_ref):
    o_ref[...] = jnp.dot(a_ref[...], b_ref[...], preferred_element_type=F32).astype(o_ref.dtype)


def _matmul(a, b, out_dtype, name, tm_pref=1024, tn_pref=1024):
    m, k = a.shape
    n = b.shape[1]
    tm, tn = _tile(m, tm_pref), _tile(n, tn_pref)
    osz = jnp.dtype(out_dtype).itemsize
    vmem = 2 * (tm * k * 2 + k * tn * 2 + tm * tn * osz) + tm * tn * 4 + (4 << 20)
    return pl.pallas_call(
        _mm_kernel,
        out_shape=jax.ShapeDtypeStruct((m, n), out_dtype),
        grid=(m // tm, n // tn),
        in_specs=[pl.BlockSpec((tm, k), lambda i, j: (i, 0)),
                  pl.BlockSpec((k, tn), lambda i, j: (0, j))],
        out_specs=pl.BlockSpec((tm, tn), lambda i, j: (i, j)),
        compiler_params=_params(("parallel", "parallel"), vmem),
        name=name,
    )(a, b)


def _da_kernel(parts_ref, lam_ref, q_ref, k_ref, v_ref, w_ref, o_ref, s_ref, p_ref, kaug_ref, l_ref, *,
               tq, hb, nq, lam_init):
    hg = pl.program_id(1)
    qi = pl.program_id(2)
    lp = lam_ref[...]
    lam = (jnp.exp(jnp.sum(lp[0:1] * lp[1:2], axis=-1, keepdims=True))
           - jnp.exp(jnp.sum(lp[2:3] * lp[3:4], axis=-1, keepdims=True)) + lam_init)
    hw = q_ref.shape[1] // hb
    dh = hw // 2
    dv = v_ref.shape[1] // hb
    seq = k_ref.shape[0]
    lane = lax.broadcasted_iota(jnp.int32, (tq, hw), 1)

    def aug_columns(pos, lane_idx, h, key_side):
        hi = (pos >> 8).astype(F32)
        lo = (pos & 255).astype(F32)
        cols = jnp.zeros(pos.shape, F32)
        for i in range(3):
            part = parts_ref[3 * h + i]
            const = (-256.0 * part, -part) if key_side else (256.0 * part, part)
            var = (6 + i, 9 + i) if key_side else (i, 3 + i)
            fixed = (i, 3 + i) if key_side else (6 + i, 9 + i)
            cols = jnp.where(lane_idx == fixed[0], const[0], cols)
            cols = jnp.where(lane_idx == fixed[1], const[1], cols)
            cols = jnp.where(lane_idx == var[0], hi, cols)
            cols = jnp.where(lane_idx == var[1], lo, cols)
        return cols.astype(BF16)

    @pl.when(qi == 0)
    def _():
        kpos = lax.broadcasted_iota(jnp.int32, (seq, LANES), 0)
        klane = lax.broadcasted_iota(jnp.int32, (seq, LANES), 1)
        for hh in range(hb):
            kaug_ref[hh] = aug_columns(kpos, klane, hg * hb + hh, True)

    def run(nblk):
        nk = nblk * tq
        qpos = lax.broadcasted_iota(jnp.int32, (tq, LANES), 0) + (nk - tq)
        qlane = lax.broadcasted_iota(jnp.int32, (tq, LANES), 1)
        row = lax.broadcasted_iota(jnp.int32, (tq, tq), 0)
        col = lax.broadcasted_iota(jnp.int32, (tq, tq), 1)
        future = jnp.concatenate([col > row, col > row], axis=0)
        for hh in range(hb):
            q = q_ref[:, hh * hw:(hh + 1) * hw]
            qs = (q.astype(F32) * (dh ** -0.5)).astype(BF16)
            zero = jnp.zeros_like(qs)
            qaug = aug_columns(qpos, qlane, hg * hb + hh, False)
            qc = jnp.concatenate(
                [jnp.concatenate([jnp.where(lane < dh, qs, zero), qaug], axis=1),
                 jnp.concatenate([jnp.where(lane >= dh, qs, zero), qaug], axis=1)], axis=0)
            k = jnp.concatenate([k_ref[0:nk, hh * hw:(hh + 1) * hw], kaug_ref[hh, 0:nk, :]], axis=1)
            s = lax.dot_general(qc, k, _NT, preferred_element_type=F32)
            last = jnp.where(future, -jnp.inf, s[:, nk - tq:])
            s = last if nblk == 1 else jnp.concatenate([s[:, :nk - tq], last], axis=1)
            s_ref[hh, :, 0:nk] = s

            for r0 in range(0, 2 * tq, SOFTMAX_ROWS):
                sc = s_ref[hh, r0:r0 + SOFTMAX_ROWS, 0:nk]
                p = jnp.exp(sc - jnp.max(sc, axis=-1, keepdims=True))
                l_ref[hh, r0:r0 + SOFTMAX_ROWS, :] = jnp.sum(p, axis=-1, keepdims=True)
                p_ref[hh, r0:r0 + SOFTMAX_ROWS, 0:nk] = p.astype(BF16)
            l = l_ref[hh]
            pv = jnp.dot(p_ref[hh, :, 0:nk], v_ref[0:nk, hh * dv:(hh + 1) * dv],
                         preferred_element_type=F32)
            o = pv[:tq] / l[:tq] - lam * (pv[tq:] / l[tq:])
            ms = jnp.mean(o * o, axis=-1, keepdims=True)
            o = o * (lax.rsqrt(ms + NORM_EPS) * (1.0 - lam_init)) * w_ref[...]
            o_ref[:, hh * dv:(hh + 1) * dv] = o.astype(o_ref.dtype)

    lax.switch(qi, [functools.partial(run, n) for n in range(1, nq + 1)])


def _diff_attention(proj, da_lambda, subln_w, bsz, seq, q_off, k_off, v_off, layer_idx):
    t = bsz * seq
    hw = 2 * DA_DH
    hb = min(2, DA_HEADS)
    ng = DA_HEADS // hb
    gw = hb * hw
    assert hw == DA_DV and q_off % gw == 0 and k_off % gw == 0 and v_off % gw == 0
    tq = _tile(seq, 256)
    nq = seq // tq
    lam_init = 0.8 - 0.6 * math.exp(-0.3 * layer_idx)
    slopes = 2.0 ** (-8.0 * jnp.arange(1, DA_HEADS + 1, dtype=F32) / DA_HEADS)
    p1 = slopes.astype(BF16).astype(F32)
    p2 = (slopes - p1).astype(BF16).astype(F32)
    p3 = (slopes - p1 - p2).astype(BF16).astype(F32)
    parts = jnp.stack([p1, p2, p3], axis=1).reshape(-1)
    qb, kb, vb = q_off // gw, k_off // gw, v_off // gw
    vmem = (2 * (2 * tq * gw * 2 + 2 * seq * gw * 2) + hb * 2 * tq * seq * 6 + hb * seq * LANES * 2
            + 3 * 2 * tq * seq * 4 + (8 << 20))
    return pl.pallas_call(
        functools.partial(_da_kernel, tq=tq, hb=hb, nq=nq, lam_init=lam_init),
        out_shape=jax.ShapeDtypeStruct((t, DA_HEADS * DA_DV), BF16),
        grid=(bsz, ng, nq),
        in_specs=[
            pl.BlockSpec(memory_space=pltpu.SMEM),
            pl.BlockSpec((4, DA_DH), lambda b, g, i: (0, 0)),
            pl.BlockSpec((tq, gw), lambda b, g, i: (b * nq + i, qb + g)),
            pl.BlockSpec((seq, gw), lambda b, g, i: (b, kb + g)),
            pl.BlockSpec((seq, gw), lambda b, g, i: (b, vb + g)),
            pl.BlockSpec((1, DA_DV), lambda b, g, i: (0, 0)),
        ],
        out_specs=pl.BlockSpec((tq, gw), lambda b, g, i: (b * nq + i, g)),
        scratch_shapes=[pltpu.VMEM((hb, 2 * tq, seq), F32), pltpu.VMEM((hb, 2 * tq, seq), BF16),
                        pltpu.VMEM((hb, seq, LANES), BF16), pltpu.VMEM((hb, 2 * tq, 1), F32)],
        compiler_params=_params(("parallel", "parallel", "arbitrary"), vmem),
        name="diffattn",
    )(parts, da_lambda.astype(F32), proj, proj, proj, subln_w.astype(F32).reshape(1, DA_DV))


def _ret_kernel(cd_ref, q_ref, k_ref, v_ref, g_ref, dm_ref, cross_ref, sd_ref, w_ref, b_ref,
                o_ref, state_ref, *, hg):
    grp = pl.program_id(1)
    c = pl.program_id(2)

    @pl.when(c == 0)
    def _():
        state_ref[...] = jnp.zeros_like(state_ref)

    for hh in range(hg):
        qh = q_ref[:, hh * RET_DK:(hh + 1) * RET_DK]
        kh = k_ref[:, hh * RET_DK:(hh + 1) * RET_DK]
        vh = v_ref[:, hh * RET_DV:(hh + 1) * RET_DV]
        st = state_ref[hh]
        inner = lax.dot_general(qh, kh, _NT, preferred_element_type=F32) * dm_ref[hh]
        out = (jnp.dot(inner.astype(BF16), vh, preferred_element_type=F32)
               + jnp.dot(qh, st.astype(BF16), preferred_element_type=F32) * cross_ref[hh])
        ks = (kh.astype(F32) * sd_ref[hh]).astype(BF16)
        state_ref[hh] = (st * cd_ref[grp * hg + hh]
                         + lax.dot_general(ks, vh, _TN, preferred_element_type=F32))
        mu = jnp.mean(out, axis=-1, keepdims=True)
        d = out - mu
        var = jnp.mean(d * d, axis=-1, keepdims=True)
        sl = slice(hh * RET_DV, (hh + 1) * RET_DV)
        y = d * lax.rsqrt(var + NORM_EPS) * w_ref[:, sl] + b_ref[:, sl]
        g = g_ref[:, sl].astype(F32)
        o_ref[:, sl] = (y * (g * _sigmoid(g))).astype(o_ref.dtype)


def _retention(proj, gn_w, gn_b, bsz, seq, q_off, k_off, v_off, g_off):
    t = bsz * seq
    nc = seq // RET_CHUNK
    hg = min(8, RET_HEADS)
    ngrp = RET_HEADS // hg
    qw, vw = hg * RET_DK, hg * RET_DV
    assert q_off % qw == 0 and k_off % qw == 0 and v_off % vw == 0 and g_off % vw == 0
    log_g = jnp.log(1.0 - 2.0 ** (-5.0 - jnp.arange(RET_HEADS, dtype=F32)))
    idx = jnp.arange(RET_CHUNK, dtype=F32)
    rel = idx[:, None] - idx[None, :]
    scale = RET_DK ** -0.5
    dmask = jnp.where(rel >= 0, jnp.exp(log_g[:, None, None] * jnp.maximum(rel, 0.0)), 0.0) * scale
    cross = jnp.exp(log_g[:, None] * (idx + 1.0))[:, :, None]
    sdec = (jnp.exp(log_g[:, None] * (RET_CHUNK - 1.0 - idx)) * scale)[:, :, None]
    cdec = jnp.exp(log_g * RET_CHUNK)
    qb, kb, vb, gb = q_off // qw, k_off // qw, v_off // vw, g_off // vw
    c_ = RET_CHUNK
    vmem = (2 * (2 * c_ * qw * 2 + 3 * c_ * vw * 2 + hg * c_ * c_ * 4 + 2 * hg * c_ * LANES * 4)
            + hg * RET_DK * RET_DV * 4 + (16 << 20))
    return pl.pallas_call(
        functools.partial(_ret_kernel, hg=hg),
        out_shape=jax.ShapeDtypeStruct((t, RET_HEADS * RET_DV), BF16),
        grid=(bsz, ngrp, nc),
        in_specs=[
            pl.BlockSpec(memory_space=pltpu.SMEM),
            pl.BlockSpec((c_, qw), lambda b, g, c: (b * nc + c, qb + g)),
            pl.BlockSpec((c_, qw), lambda b, g, c: (b * nc + c, kb + g)),
            pl.BlockSpec((c_, vw), lambda b, g, c: (b * nc + c, vb + g)),
            pl.BlockSpec((c_, vw), lambda b, g, c: (b * nc + c, gb + g)),
            pl.BlockSpec((hg, c_, c_), lambda b, g, c: (g, 0, 0)),
            pl.BlockSpec((hg, c_, 1), lambda b, g, c: (g, 0, 0)),
            pl.BlockSpec((hg, c_, 1), lambda b, g, c: (g, 0, 0)),
            pl.BlockSpec((1, vw), lambda b, g, c: (0, g)),
            pl.BlockSpec((1, vw), lambda b, g, c: (0, g)),
        ],
        out_specs=pl.BlockSpec((c_, vw), lambda b, g, c: (b * nc + c, g)),
        scratch_shapes=[pltpu.VMEM((hg, RET_DK, RET_DV), F32)],
        compiler_params=_params(("parallel", "parallel", "arbitrary"), vmem),
        name="retention",
    )(cdec, proj, proj, proj, proj, dmask, cross, sdec,
      gn_w.astype(F32).reshape(1, -1), gn_b.astype(F32).reshape(1, -1))


def _merge_kernel(a_ref, wd_ref, r_ref, wr_ref, ga_ref, gr_ref, o_ref):
    a2 = jnp.dot(a_ref[...], wd_ref[...], preferred_element_type=F32)
    r2 = jnp.dot(r_ref[...], wr_ref[...], preferred_element_type=F32)
    ga = ga_ref[...].astype(F32)
    gr = gr_ref[...].astype(F32)
    o_ref[...] = (_sigmoid(ga) * a2 + _sigmoid(gr) * r2).astype(o_ref.dtype)


def _merge(a, wd, r, wr, proj, ga_off, gr_off, d_model):
    t = a.shape[0]
    ka, kr = a.shape[1], r.shape[1]
    tm, tn = _tile(t, 512), _tile(d_model, 1024)
    assert ga_off % tn == 0 and gr_off % tn == 0
    gab, grb = ga_off // tn, gr_off // tn
    vmem = 2 * ((tm + tn) * (ka + kr) * 2 + 3 * tm * tn * 2) + 3 * tm * tn * 4 + (4 << 20)
    return pl.pallas_call(
        _merge_kernel,
        out_shape=jax.ShapeDtypeStruct((t, d_model), BF16),
        grid=(t // tm, d_model // tn),
        in_specs=[
            pl.BlockSpec((tm, ka), lambda i, j: (i, 0)),
            pl.BlockSpec((ka, tn), lambda i, j: (0, j)),
            pl.BlockSpec((tm, kr), lambda i, j: (i, 0)),
            pl.BlockSpec((kr, tn), lambda i, j: (0, j)),
            pl.BlockSpec((tm, tn), lambda i, j: (i, gab + j)),
            pl.BlockSpec((tm, tn), lambda i, j: (i, grb + j)),
        ],
        out_specs=pl.BlockSpec((tm, tn), lambda i, j: (i, j)),
        compiler_params=_params(("parallel", "parallel"), vmem),
        name="merge",
    )(a, wd, r, wr, proj, proj)


def _layer_norm_slabs(acc_ref, nj, tn, lw_ref, lb_ref, write):
    d = nj * tn
    tot = acc_ref[0].sum(axis=-1, keepdims=True)
    for jj in range(1, nj):
        tot = tot + acc_ref[jj].sum(axis=-1, keepdims=True)
    mu = tot / d
    sq = jnp.square(acc_ref[0] - mu).sum(axis=-1, keepdims=True)
    for jj in range(1, nj):
        sq = sq + jnp.square(acc_ref[jj] - mu).sum(axis=-1, keepdims=True)
    rstd = lax.rsqrt(sq / d + NORM_EPS)
    for jj in range(nj):
        sl = slice(jj * tn, (jj + 1) * tn)
        write(sl, (acc_ref[jj] - mu) * rstd * lw_ref[:, sl] + lb_ref[:, sl])


def _outln_kernel(m_ref, w_ref, x_ref, lw_ref, lb_ref, of_ref, ob_ref, acc_ref, *, nj, tn):
    j = pl.program_id(1)
    acc_ref[j] = DEEPNORM_ALPHA * x_ref[...] + jnp.dot(m_ref[...], w_ref[...],
                                                        preferred_element_type=F32)

    @pl.when(j == nj - 1)
    def _():
        def write(sl, y):
            of_ref[:, sl] = y
            ob_ref[:, sl] = y.astype(ob_ref.dtype)
        _layer_norm_slabs(acc_ref, nj, tn, lw_ref, lb_ref, write)


def _out_ln(merged, w_out, x2d, ln_w, ln_b):
    t, d = x2d.shape
    tm, tn = _tile(t, 256), _tile(d, 1024)
    nj = d // tn
    vmem = (2 * (tm * d * 2 + d * tn * 2 + tm * tn * 4 + tm * d * 4 + tm * d * 2)
            + tm * d * 4 + 3 * tm * tn * 4 + (4 << 20))
    return pl.pallas_call(
        functools.partial(_outln_kernel, nj=nj, tn=tn),
        out_shape=(jax.ShapeDtypeStruct((t, d), F32), jax.ShapeDtypeStruct((t, d), BF16)),
        grid=(t // tm, nj),
        in_specs=[
            pl.BlockSpec((tm, d), lambda i, j: (i, 0)),
            pl.BlockSpec((d, tn), lambda i, j: (0, j)),
            pl.BlockSpec((tm, tn), lambda i, j: (i, j)),
            pl.BlockSpec((1, d), lambda i, j: (0, 0)),
            pl.BlockSpec((1, d), lambda i, j: (0, 0)),
        ],
        out_specs=(pl.BlockSpec((tm, d), lambda i, j: (i, 0)),
                   pl.BlockSpec((tm, d), lambda i, j: (i, 0))),
        scratch_shapes=[pltpu.VMEM((nj, tm, tn), F32)],
        compiler_params=_params(("parallel", "arbitrary"), vmem),
        name="outln",
    )(merged, w_out, x2d, ln_w.astype(F32).reshape(1, d), ln_b.astype(F32).reshape(1, d))


_NOT_TOP = 99.0


def _top16(s):
    nk, tl = s.shape
    key = lax.broadcasted_iota(jnp.int32, (nk, tl), 0).astype(F32)
    rank = jnp.full((nk, tl), _NOT_TOP, F32)
    vals = []
    for r in range(PEER_TOPK):
        m = jnp.max(s, axis=0, keepdims=True)
        first = jnp.min(jnp.where(s == m, key, float(nk)), axis=0, keepdims=True)
        hit = key == first
        rank = jnp.where(hit, float(r), rank)
        s = jnp.where(hit, -jnp.inf, s)
        vals.append(m)
    return jnp.concatenate(vals, axis=0), rank


def _route_kernel(x_ref, wq_ref, keys_ref, r2_ref, e2_ref, n_ref, c_ref, q_scr):
    h = pl.program_id(1)
    ngrp = q_scr.shape[0]
    dk = q_scr.shape[2]

    @pl.when(h == 0)
    def _():
        q = jnp.dot(x_ref[...], wq_ref[...], preferred_element_type=F32)
        for g in range(ngrp):
            q_scr[g] = q[:, g * dk:(g + 1) * dk].astype(q_scr.dtype)

    s1 = lax.dot_general(keys_ref[2 * h], q_scr[2 * h], _NT, preferred_element_type=F32)
    s2 = lax.dot_general(keys_ref[2 * h + 1], q_scr[2 * h + 1], _NT, preferred_element_type=F32)
    a, rank1 = _top16(s1)
    b, rank2 = _top16(s2)
    tl = s1.shape[1]
    k = PEER_TOPK
    ea = jnp.exp(a - a[0:1])
    eb = jnp.exp(b - b[0:1])

    half = k // 2
    i_k = lax.broadcasted_iota(jnp.int32, (k, tl), 0).astype(F32)
    i_h = lax.broadcasted_iota(jnp.int32, (half, tl), 0).astype(F32)
    cand = [a[0:1] + b]
    prod = [ea[0:1] * eb]
    pos = [i_k]
    for r1 in range(1, half):
        cand.append(a[r1:r1 + 1] + b[0:half])
        prod.append(ea[r1:r1 + 1] * eb[0:half])
        pos.append(i_h + float(r1 * k))
    cand.append(a[half:k] + b[0:1])
    prod.append(ea[half:k] * eb[0:1])
    pos.append((i_h + float(half)) * float(k))
    cand = jnp.concatenate(cand, axis=0)
    prod = jnp.concatenate(prod, axis=0)
    pos = jnp.concatenate(pos, axis=0)
    picked = jnp.zeros_like(cand)
    for _ in range(k):
        m = jnp.max(cand, axis=0, keepdims=True)
        first = jnp.min(jnp.where(cand == m, pos, float(k * k)), axis=0, keepdims=True)
        hit = pos == first
        picked = jnp.where(hit, 1.0, picked)
        cand = jnp.where(hit, -jnp.inf, cand)
    z = jnp.sum(picked * prod, axis=0, keepdims=True)

    counts = [jnp.sum(picked[0:k], axis=0, keepdims=True)]
    for r1 in range(1, half):
        lo = k + (r1 - 1) * half
        counts.append(jnp.sum(picked[lo:lo + half], axis=0, keepdims=True))
    lo = k + (half - 1) * half
    for r in range(half):
        counts.append(picked[lo + r:lo + r + 1])
    n = jnp.zeros_like(s1)
    for r1 in range(k):
        n = jnp.where(rank1 == float(r1), counts[r1], n)

    r2_ref[0] = rank2.astype(r2_ref.dtype)
    e2_ref[0] = jnp.exp(s2 - b[0:1]).astype(e2_ref.dtype)
    n_ref[0] = n
    c_ref[0] = jnp.exp(s1 - a[0:1]) / z


def _route(x1b, wq, keys):
    t, d = x1b.shape
    nq = wq.shape[1]
    ngrp = 2 * PEER_HEADS
    assert nq == ngrp * PEER_DHALF and PEER_TOPK == 16 and PEER_NKEYS % 8 == 0
    tl = _tile(t, 256)
    shp = jax.ShapeDtypeStruct((PEER_HEADS, PEER_NKEYS, t), F32)
    shp_b = jax.ShapeDtypeStruct((PEER_HEADS, PEER_NKEYS, t), BF16)
    ospec = pl.BlockSpec((1, PEER_NKEYS, tl), lambda i, h: (h, 0, i))
    vmem = (2 * tl * d * 2 + d * nq * 2 + 2 * ngrp * PEER_NKEYS * PEER_DHALF * 2
            + 8 * PEER_NKEYS * tl * 4 + ngrp * tl * PEER_DHALF * 2 + tl * nq * 4 + (16 << 20))
    return pl.pallas_call(
        _route_kernel,
        out_shape=(shp_b, shp_b, shp, shp),
        grid=(t // tl, PEER_HEADS),
        in_specs=[
            pl.BlockSpec((tl, d), lambda i, h: (i, 0)),
            pl.BlockSpec((d, nq), lambda i, h: (0, 0), pipeline_mode=pl.Buffered(1)),
            pl.BlockSpec((ngrp, PEER_NKEYS, PEER_DHALF), lambda i, h: (0, 0, 0)),
        ],
        out_specs=(ospec, ospec, ospec, ospec),
        scratch_shapes=[pltpu.VMEM((ngrp, tl, PEER_DHALF), BF16)],
        compiler_params=_params(("parallel", "arbitrary"), vmem),
        name="route",
    )(x1b, wq, keys)


def _peer_kernel(x_ref, u_ref, vt_ref, r2_ref, e2_ref, n_ref, c_ref, yt_ref, g_ref, *, rows):
    j = pl.program_id(1)
    nk = PEER_NKEYS

    @pl.when(j == 0)
    def _():
        yt_ref[...] = jnp.zeros_like(yt_ref)

    for a in range(rows):
        i1 = j * rows + a
        g = None
        for h in range(PEER_HEADS):
            nrow = n_ref[h, pl.ds(i1, 1), :].astype(BF16)
            crow = c_ref[h, pl.ds(i1, 1), :].astype(BF16)
            term = jnp.where(r2_ref[h] < nrow, e2_ref[h] * crow, jnp.zeros((), BF16))
            g = term if g is None else g + term
        g_ref[a * nk:(a + 1) * nk, :] = g

    te = rows * nk
    hte = te // 2
    x = x_ref[...]
    acc = None
    for s in range(2):
        sl = slice(s * hte, (s + 1) * hte)
        ht = lax.dot_general(u_ref[sl, :], x, _NT, preferred_element_type=F32)
        gelu = ht * (lax.erf(ht * (2.0 ** -0.5)) + 1.0) * 0.5
        act_t = gelu.astype(BF16) * g_ref[sl, :]
        part = jnp.dot(vt_ref[:, sl], act_t, preferred_element_type=F32)
        acc = part if acc is None else acc + part
    yt_ref[...] += acc


def _peer_dense(x1b, u, vt, r2, e2, nn, cc):
    t, d = x1b.shape
    ne = u.shape[0]
    assert ne == PEER_NKEYS * PEER_NKEYS and vt.shape == (d, ne)
    tm = _tile(t, 512)
    rows = 4
    te = rows * PEER_NKEYS
    hk = PEER_HEADS * PEER_NKEYS
    bspec = pl.BlockSpec((PEER_HEADS, PEER_NKEYS, tm), lambda i, j: (0, 0, i),
                         pipeline_mode=pl.Buffered(1))
    vmem = (tm * d * 2 + 4 * te * d * 2 + 2 * hk * tm * 2 + 2 * hk * tm * 4 + 2 * d * tm * 4
            + te * tm * 2 + 3 * te * tm * 4 + (4 << 20))
    return pl.pallas_call(
        functools.partial(_peer_kernel, rows=rows),
        out_shape=jax.ShapeDtypeStruct((d, t), F32),
        grid=(t // tm, ne // te),
        in_specs=[
            pl.BlockSpec((tm, d), lambda i, j: (i, 0), pipeline_mode=pl.Buffered(1)),
            pl.BlockSpec((te, d), lambda i, j: (j, 0)),
            pl.BlockSpec((d, te), lambda i, j: (0, j)),
            bspec, bspec, bspec, bspec,
        ],
        out_specs=pl.BlockSpec((d, tm), lambda i, j: (0, i)),
        scratch_shapes=[pltpu.VMEM((te, tm), BF16)],
        compiler_params=_params(("parallel", "arbitrary"), vmem),
        name="peer",
    )(x1b, u, vt, r2, e2, nn, cc)


def _ln2_kernel(x_ref, yt_ref, w_ref, b_ref, o_ref):
    z = DEEPNORM_ALPHA * x_ref[...] + yt_ref[...].T
    mu = jnp.mean(z, axis=-1, keepdims=True)
    d = z - mu
    var = jnp.mean(d * d, axis=-1, keepdims=True)
    o_ref[...] = (d * lax.rsqrt(var + NORM_EPS) * w_ref[...] + b_ref[...]).astype(o_ref.dtype)


def _ln2(x1f, yt, ln_w, ln_b):
    t, d = x1f.shape
    tm = _tile(t, 256)
    vmem = 2 * 3 * tm * d * 4 + 3 * tm * d * 4 + (4 << 20)
    return pl.pallas_call(
        _ln2_kernel,
        out_shape=jax.ShapeDtypeStruct((t, d), F32),
        grid=(t // tm,),
        in_specs=[
            pl.BlockSpec((tm, d), lambda i: (i, 0)),
            pl.BlockSpec((d, tm), lambda i: (0, i)),
            pl.BlockSpec((1, d), lambda i: (0, 0)),
            pl.BlockSpec((1, d), lambda i: (0, 0)),
        ],
        out_specs=pl.BlockSpec((tm, d), lambda i: (i, 0)),
        compiler_params=_params(("parallel",), vmem),
        name="ln2",
    )(x1f, yt, ln_w.astype(F32).reshape(1, d), ln_b.astype(F32).reshape(1, d))


def kernel(x, w_in, da_lambda, da_subln_w, w_da_out, ret_gn_w, ret_gn_b, w_ret_out, w_out,
           ln1_w, ln1_b, peer_w_query, peer_sub_keys, peer_u, peer_v, ln2_w, ln2_b):
    bsz, seq, d = x.shape
    t = bsz * seq
    da_qk_w, da_v_w = DA_HEADS * 2 * DA_DH, DA_HEADS * DA_DV
    ret_qk_w, ret_v_w = RET_HEADS * RET_DK, RET_HEADS * RET_DV
    splits = (da_qk_w, da_qk_w, da_v_w, ret_qk_w, ret_qk_w, ret_v_w, ret_v_w, d, d)
    offs = [0]
    for s in splits:
        offs.append(offs[-1] + s)
    o_q, o_k, o_v, o_rq, o_rk, o_rv, o_rg, o_ga, o_gr = offs[:-1]

    xf = x.reshape(t, d)
    for l in range(DEPTH):
        xb = xf.astype(BF16)
        proj = _matmul(xb, w_in[l].astype(BF16), BF16, "proj")
        a = _diff_attention(proj, da_lambda[l], da_subln_w[l], bsz, seq, o_q, o_k, o_v, l)
        r = _retention(proj, ret_gn_w[l], ret_gn_b[l], bsz, seq, o_rq, o_rk, o_rv, o_rg)
        merged = _merge(a, w_da_out[l].astype(BF16), r, w_ret_out[l].astype(BF16), proj, o_ga, o_gr, d)
        x1f, x1b = _out_ln(merged, w_out[l].astype(BF16), xf, ln1_w[l], ln1_b[l])
        keys = peer_sub_keys[l].astype(BF16).reshape(2 * PEER_HEADS, PEER_NKEYS, PEER_DHALF)
        r2, e2, nn, cc = _route(x1b, peer_w_query[l].astype(BF16), keys)
        yt = _peer_dense(x1b, peer_u[l].astype(BF16), peer_v[l].T.astype(BF16), r2, e2, nn, cc)
        xf = _ln2(x1f, yt, ln2_w[l], ln2_b[l])
    return xf.reshape(bsz, seq, d)
```

```python
import functools
import math

import jax
import jax.numpy as jnp
from jax import lax
from jax.experimental import pallas as pl
from jax.experimental.pallas import tpu as pltpu

DEPTH = 1
DA_HEADS = 16
DA_DH = 64
DA_DV = 2 * DA_DH
RET_HEADS = 16
RET_DK = 128
RET_DV = 256
RET_CHUNK = 128
PEER_HEADS = 8
PEER_NKEYS = 128
PEER_DHALF = 128
PEER_TOPK = 16
NORM_EPS = 1e-5
DEEPNORM_ALPHA = (2.0 * DEPTH) ** 0.25

V7X_VMEM_BYTES = 64 * 1024 * 1024
LANES = 128
SOFTMAX_ROWS = 128
F32 = jnp.float32
BF16 = jnp.bfloat16

_NT = (((1,), (1,)), ((), ()))
_TN = (((0,), (0,)), ((), ()))


def _tile(n, pref):
    if n <= pref:
        return n
    t = pref - pref % LANES
    while t >= LANES:
        if n % t == 0:
            return t
        t -= LANES
    raise ValueError(f"no tile for {n} <= {pref}")


def _params(sem, vmem_bytes, flags=None):
    limit = min(int(vmem_bytes), V7X_VMEM_BYTES - 4 * 1024 * 1024)
    return pltpu.CompilerParams(dimension_semantics=sem, vmem_limit_bytes=limit, flags=flags)


def _sigmoid(x):
    return 1.0 / (1.0 + jnp.exp(-x))


def _mm_kernel(a_ref, b_ref, o_ref):
    o_ref[...] = jnp.dot(a_ref[...], b_ref[...], preferred_element_type=F32).astype(o_ref.dtype)


def _matmul(a, b, out_dtype, name, tm_pref=1024, tn_pref=1024):
    m, k = a.shape
    n = b.shape[1]
    tm, tn = _tile(m, tm_pref), _tile(n, tn_pref)
    osz = jnp.dtype(out_dtype).itemsize
    vmem = 2 * (tm * k * 2 + k * tn * 2 + tm * tn * osz) + tm * tn * 4 + (4 << 20)
    return pl.pallas_call(
        _mm_kernel,
        out_shape=jax.ShapeDtypeStruct((m, n), out_dtype),
        grid=(m // tm, n // tn),
        in_specs=[pl.BlockSpec((tm, k), lambda i, j: (i, 0)),
                  pl.BlockSpec((k, tn), lambda i, j: (0, j))],
        out_specs=pl.BlockSpec((tm, tn), lambda i, j: (i, j)),
        compiler_params=_params(("parallel", "parallel"), vmem),
        name=name,
    )(a, b)


def _da_kernel(parts_ref, lam_ref, q_ref, k_ref, v_ref, w_ref, o_ref, s_ref, p_ref, kaug_ref, l_ref, *,
               tq, hb, nq, lam_init):
    hg = pl.program_id(1)
    qi = pl.program_id(2)
    lp = lam_ref[...]
    lam = (jnp.exp(jnp.sum(lp[0:1] * lp[1:2], axis=-1, keepdims=True))
           - jnp.exp(jnp.sum(lp[2:3] * lp[3:4], axis=-1, keepdims=True)) + lam_init)
    hw = q_ref.shape[1] // hb
    dh = hw // 2
    dv = v_ref.shape[1] // hb
    seq = k_ref.shape[0]
    lane = lax.broadcasted_iota(jnp.int32, (tq, hw), 1)

    def aug_columns(pos, lane_idx, h, key_side):
        hi = (pos >> 8).astype(F32)
        lo = (pos & 255).astype(F32)
        cols = jnp.zeros(pos.shape, F32)
        for i in range(3):
            part = parts_ref[3 * h + i]
            const = (-256.0 * part, -part) if key_side else (256.0 * part, part)
            var = (6 + i, 9 + i) if key_side else (i, 3 + i)
            fixed = (i, 3 + i) if key_side else (6 + i, 9 + i)
            cols = jnp.where(lane_idx == fixed[0], const[0], cols)
            cols = jnp.where(lane_idx == fixed[1], const[1], cols)
            cols = jnp.where(lane_idx == var[0], hi, cols)
            cols = jnp.where(lane_idx == var[1], lo, cols)
        return cols.astype(BF16)

    @pl.when(qi == 0)
    def _():
        kpos = lax.broadcasted_iota(jnp.int32, (seq, LANES), 0)
        klane = lax.broadcasted_iota(jnp.int32, (seq, LANES), 1)
        for hh in range(hb):
            kaug_ref[hh] = aug_columns(kpos, klane, hg * hb + hh, True)

    def run(nblk):
        nk = nblk * tq
        qpos = lax.broadcasted_iota(jnp.int32, (tq, LANES), 0) + (nk - tq)
        qlane = lax.broadcasted_iota(jnp.int32, (tq, LANES), 1)
        row = lax.broadcasted_iota(jnp.int32, (tq, tq), 0)
        col = lax.broadcasted_iota(jnp.int32, (tq, tq), 1)
        future = jnp.concatenate([col > row, col > row], axis=0)
        for hh in range(hb):
            q = q_ref[:, hh * hw:(hh + 1) * hw]
            qs = (q.astype(F32) * (dh ** -0.5)).astype(BF16)
            zero = jnp.zeros_like(qs)
            qaug = aug_columns(qpos, qlane, hg * hb + hh, False)
            qc = jnp.concatenate(
                [jnp.concatenate([jnp.where(lane < dh, qs, zero), qaug], axis=1),
                 jnp.concatenate([jnp.where(lane >= dh, qs, zero), qaug], axis=1)], axis=0)
            k = jnp.concatenate([k_ref[0:nk, hh * hw:(hh + 1) * hw], kaug_ref[hh, 0:nk, :]], axis=1)
            s = lax.dot_general(qc, k, _NT, preferred_element_type=F32)
            last = jnp.where(future, -jnp.inf, s[:, nk - tq:])
            s = last if nblk == 1 else jnp.concatenate([s[:, :nk - tq], last], axis=1)
            s_ref[hh, :, 0:nk] = s

            for r0 in range(0, 2 * tq, SOFTMAX_ROWS):
                sc = s_ref[hh, r0:r0 + SOFTMAX_ROWS, 0:nk]
                p = jnp.exp(sc - jnp.max(sc, axis=-1, keepdims=True))
                l_ref[hh, r0:r0 + SOFTMAX_ROWS, :] = jnp.sum(p, axis=-1, keepdims=True)
                p_ref[hh, r0:r0 + SOFTMAX_ROWS, 0:nk] = p.astype(BF16)
            l = l_ref[hh]
            pv = jnp.dot(p_ref[hh, :, 0:nk], v_ref[0:nk, hh * dv:(hh + 1) * dv],
                         preferred_element_type=F32)
            o = pv[:tq] / l[:tq] - lam * (pv[tq:] / l[tq:])
            ms = jnp.mean(o * o, axis=-1, keepdims=True)
            o = o * (lax.rsqrt(ms + NORM_EPS) * (1.0 - lam_init)) * w_ref[...]
            o_ref[:, hh * dv:(hh + 1) * dv] = o.astype(o_ref.dtype)

    lax.switch(qi, [functools.partial(run, n) for n in range(1, nq + 1)])


def _diff_attention(proj, da_lambda, subln_w, bsz, seq, q_off, k_off, v_off, layer_idx):
    t = bsz * seq
    hw = 2 * DA_DH
    hb = min(2, DA_HEADS)
    ng = DA_HEADS // hb
    gw = hb * hw
    assert hw == DA_DV and q_off % gw == 0 and k_off % gw == 0 and v_off % gw == 0
    tq = _tile(seq, 256)
    nq = seq // tq
    lam_init = 0.8 - 0.6 * math.exp(-0.3 * layer_idx)
    slopes = 2.0 ** (-8.0 * jnp.arange(1, DA_HEADS + 1, dtype=F32) / DA_HEADS)
    p1 = slopes.astype(BF16).astype(F32)
    p2 = (slopes - p1).astype(BF16).astype(F32)
    p3 = (slopes - p1 - p2).astype(BF16).astype(F32)
    parts = jnp.stack([p1, p2, p3], axis=1).reshape(-1)
    qb, kb, vb = q_off // gw, k_off // gw, v_off // gw
    vmem = (2 * (2 * tq * gw * 2 + 2 * seq * gw * 2) + hb * 2 * tq * seq * 6 + hb * seq * LANES * 2
            + 3 * 2 * tq * seq * 4 + (8 << 20))
    return pl.pallas_call(
        functools.partial(_da_kernel, tq=tq, hb=hb, nq=nq, lam_init=lam_init),
        out_shape=jax.ShapeDtypeStruct((t, DA_HEADS * DA_DV), BF16),
        grid=(bsz, ng, nq),
        in_specs=[
            pl.BlockSpec(memory_space=pltpu.SMEM),
            pl.BlockSpec((4, DA_DH), lambda b, g, i: (0, 0)),
            pl.BlockSpec((tq, gw), lambda b, g, i: (b * nq + i, qb + g)),
            pl.BlockSpec((seq, gw), lambda b, g, i: (b, kb + g)),
            pl.BlockSpec((seq, gw), lambda b, g, i: (b, vb + g)),
            pl.BlockSpec((1, DA_DV), lambda b, g, i: (0, 0)),
        ],
        out_specs=pl.BlockSpec((tq, gw), lambda b, g, i: (b * nq + i, g)),
        scratch_shapes=[pltpu.VMEM((hb, 2 * tq, seq), F32), pltpu.VMEM((hb, 2 * tq, seq), BF16),
                        pltpu.VMEM((hb, seq, LANES), BF16), pltpu.VMEM((hb, 2 * tq, 1), F32)],
        compiler_params=_params(("parallel", "parallel", "arbitrary"), vmem),
        name="diffattn",
    )(parts, da_lambda.astype(F32), proj, proj, proj, subln_w.astype(F32).reshape(1, DA_DV))


def _ret_kernel(cd_ref, q_ref, k_ref, v_ref, g_ref, dm_ref, cross_ref, sd_ref, w_ref, b_ref,
                o_ref, state_ref, *, hg):
    grp = pl.program_id(1)
    c = pl.program_id(2)

    @pl.when(c == 0)
    def _():
        state_ref[...] = jnp.zeros_like(state_ref)

    for hh in range(hg):
        qh = q_ref[:, hh * RET_DK:(hh + 1) * RET_DK]
        kh = k_ref[:, hh * RET_DK:(hh + 1) * RET_DK]
        vh = v_ref[:, hh * RET_DV:(hh + 1) * RET_DV]
        st = state_ref[hh]
        inner = lax.dot_general(qh, kh, _NT, preferred_element_type=F32) * dm_ref[hh]
        out = (jnp.dot(inner.astype(BF16), vh, preferred_element_type=F32)
               + jnp.dot(qh, st.astype(BF16), preferred_element_type=F32) * cross_ref[hh])
        ks = (kh.astype(F32) * sd_ref[hh]).astype(BF16)
        state_ref[hh] = (st * cd_ref[grp * hg + hh]
                         + lax.dot_general(ks, vh, _TN, preferred_element_type=F32))
        mu = jnp.mean(out, axis=-1, keepdims=True)
        d = out - mu
        var = jnp.mean(d * d, axis=-1, keepdims=True)
        sl = slice(hh * RET_DV, (hh + 1) * RET_DV)
        y = d * lax.rsqrt(var + NORM_EPS) * w_ref[:, sl] + b_ref[:, sl]
        g = g_ref[:, sl].astype(F32)
        o_ref[:, sl] = (y * (g * _sigmoid(g))).astype(o_ref.dtype)


def _retention(proj, gn_w, gn_b, bsz, seq, q_off, k_off, v_off, g_off):
    t = bsz * seq
    nc = seq // RET_CHUNK
    hg = min(8, RET_HEADS)
    ngrp = RET_HEADS // hg
    qw, vw = hg * RET_DK, hg * RET_DV
    assert q_off % qw == 0 and k_off % qw == 0 and v_off % vw == 0 and g_off % vw == 0
    log_g = jnp.log(1.0 - 2.0 ** (-5.0 - jnp.arange(RET_HEADS, dtype=F32)))
    idx = jnp.arange(RET_CHUNK, dtype=F32)
    rel = idx[:, None] - idx[None, :]
    scale = RET_DK ** -0.5
    dmask = jnp.where(rel >= 0, jnp.exp(log_g[:, None, None] * jnp.maximum(rel, 0.0)), 0.0) * scale
    cross = jnp.exp(log_g[:, None] * (idx + 1.0))[:, :, None]
    sdec = (jnp.exp(log_g[:, None] * (RET_CHUNK - 1.0 - idx)) * scale)[:, :, None]
    cdec = jnp.exp(log_g * RET_CHUNK)
    qb, kb, vb, gb = q_off // qw, k_off // qw, v_off // vw, g_off // vw
    c_ = RET_CHUNK
    vmem = (2 * (2 * c_ * qw * 2 + 3 * c_ * vw * 2 + hg * c_ * c_ * 4 + 2 * hg * c_ * LANES * 4)
            + hg * RET_DK * RET_DV * 4 + (16 << 20))
    return pl.pallas_call(
        functools.partial(_ret_kernel, hg=hg),
        out_shape=jax.ShapeDtypeStruct((t, RET_HEADS * RET_DV), BF16),
        grid=(bsz, ngrp, nc),
        in_specs=[
            pl.BlockSpec(memory_space=pltpu.SMEM),
            pl.BlockSpec((c_, qw), lambda b, g, c: (b * nc + c, qb + g)),
            pl.BlockSpec((c_, qw), lambda b, g, c: (b * nc + c, kb + g)),
            pl.BlockSpec((c_, vw), lambda b, g, c: (b * nc + c, vb + g)),
            pl.BlockSpec((c_, vw), lambda b, g, c: (b * nc + c, gb + g)),
            pl.BlockSpec((hg, c_, c_), lambda b, g, c: (g, 0, 0)),
            pl.BlockSpec((hg, c_, 1), lambda b, g, c: (g, 0, 0)),
            pl.BlockSpec((hg, c_, 1), lambda b, g, c: (g, 0, 0)),
            pl.BlockSpec((1, vw), lambda b, g, c: (0, g)),
            pl.BlockSpec((1, vw), lambda b, g, c: (0, g)),
        ],
        out_specs=pl.BlockSpec((c_, vw), lambda b, g, c: (b * nc + c, g)),
        scratch_shapes=[pltpu.VMEM((hg, RET_DK, RET_DV), F32)],
        compiler_params=_params(("parallel", "parallel", "arbitrary"), vmem),
        name="retention",
    )(cdec, proj, proj, proj, proj, dmask, cross, sdec,
      gn_w.astype(F32).reshape(1, -1), gn_b.astype(F32).reshape(1, -1))


def _merge_kernel(a_ref, wd_ref, r_ref, wr_ref, ga_ref, gr_ref, o_ref):
    a2 = jnp.dot(a_ref[...], wd_ref[...], preferred_element_type=F32)
    r2 = jnp.dot(r_ref[...], wr_ref[...], preferred_element_type=F32)
    ga = ga_ref[...].astype(F32)
    gr = gr_ref[...].astype(F32)
    o_ref[...] = (_sigmoid(ga) * a2 + _sigmoid(gr) * r2).astype(o_ref.dtype)


def _merge(a, wd, r, wr, proj, ga_off, gr_off, d_model):
    t = a.shape[0]
    ka, kr = a.shape[1], r.shape[1]
    tm, tn = _tile(t, 512), _tile(d_model, 1024)
    assert ga_off % tn == 0 and gr_off % tn == 0
    gab, grb = ga_off // tn, gr_off // tn
    vmem = 2 * ((tm + tn) * (ka + kr) * 2 + 3 * tm * tn * 2) + 3 * tm * tn * 4 + (4 << 20)
    return pl.pallas_call(
        _merge_kernel,
        out_shape=jax.ShapeDtypeStruct((t, d_model), BF16),
        grid=(t // tm, d_model // tn),
        in_specs=[
            pl.BlockSpec((tm, ka), lambda i, j: (i, 0)),
            pl.BlockSpec((ka, tn), lambda i, j: (0, j)),
            pl.BlockSpec((tm, kr), lambda i, j: (i, 0)),
            pl.BlockSpec((kr, tn), lambda i, j: (0, j)),
            pl.BlockSpec((tm, tn), lambda i, j: (i, gab + j)),
            pl.BlockSpec((tm, tn), lambda i, j: (i, grb + j)),
        ],
        out_specs=pl.BlockSpec((tm, tn), lambda i, j: (i, j)),
        compiler_params=_params(("parallel", "parallel"), vmem),
        name="merge",
    )(a, wd, r, wr, proj, proj)


def _outln_kernel(m_ref, w_ref, x_ref, lw_ref, lb_ref, of_ref, ob_ref, *, nj, tn):
    j = pl.program_id(1)
    z = DEEPNORM_ALPHA * x_ref[...] + jnp.dot(m_ref[...], w_ref[...], preferred_element_type=F32)

    def store(jj):
        def f():
            of_ref[:, jj * tn:(jj + 1) * tn] = z
        return f

    lax.switch(j, [store(jj) for jj in range(nj)])

    @pl.when(j == nj - 1)
    def _():
        d = nj * tn
        cols = [slice(jj * tn, (jj + 1) * tn) for jj in range(nj)]
        tot = of_ref[:, cols[0]].sum(axis=-1, keepdims=True)
        for sl in cols[1:]:
            tot = tot + of_ref[:, sl].sum(axis=-1, keepdims=True)
        mu = tot / d
        sq = jnp.square(of_ref[:, cols[0]] - mu).sum(axis=-1, keepdims=True)
        for sl in cols[1:]:
            sq = sq + jnp.square(of_ref[:, sl] - mu).sum(axis=-1, keepdims=True)
        rstd = lax.rsqrt(sq / d + NORM_EPS)
        for sl in cols:
            y = (of_ref[:, sl] - mu) * rstd * lw_ref[:, sl] + lb_ref[:, sl]
            of_ref[:, sl] = y
            ob_ref[:, sl] = y.astype(ob_ref.dtype)


def _out_ln(merged, w_out, x2d, ln_w, ln_b):
    t, d = x2d.shape
    tm, tn = _tile(t, 512), _tile(d, 512)
    nj = d // tn
    vmem = (2 * (tm * d * 2 + d * tn * 2 + tm * tn * 4 + tm * d * 4 + tm * d * 2)
            + 3 * tm * tn * 4 + (4 << 20))
    return pl.pallas_call(
        functools.partial(_outln_kernel, nj=nj, tn=tn),
        out_shape=(jax.ShapeDtypeStruct((t, d), F32), jax.ShapeDtypeStruct((t, d), BF16)),
        grid=(t // tm, nj),
        in_specs=[
            pl.BlockSpec((tm, d), lambda i, j: (i, 0)),
            pl.BlockSpec((d, tn), lambda i, j: (0, j)),
            pl.BlockSpec((tm, tn), lambda i, j: (i, j)),
            pl.BlockSpec((1, d), lambda i, j: (0, 0)),
            pl.BlockSpec((1, d), lambda i, j: (0, 0)),
        ],
        out_specs=(pl.BlockSpec((tm, d), lambda i, j: (i, 0)),
                   pl.BlockSpec((tm, d), lambda i, j: (i, 0))),
        compiler_params=_params(("parallel", "arbitrary"), vmem),
        name="outln",
    )(merged, w_out, x2d, ln_w.astype(F32).reshape(1, d), ln_b.astype(F32).reshape(1, d))


_NOT_TOP = 99.0


def _top16(s):
    nk, tl = s.shape
    key = lax.broadcasted_iota(jnp.int32, (nk, tl), 0).astype(F32)
    rank = jnp.full((nk, tl), _NOT_TOP, F32)
    vals = []
    for r in range(PEER_TOPK):
        m = jnp.max(s, axis=0, keepdims=True)
        first = jnp.min(jnp.where(s == m, key, float(nk)), axis=0, keepdims=True)
        hit = key == first
        rank = jnp.where(hit, float(r), rank)
        s = jnp.where(hit, -jnp.inf, s)
        vals.append(m)
    return jnp.concatenate(vals, axis=0), rank


def _route_head(s1, s2):
    tl = s1.shape[1]
    vals, rank = _top16(jnp.concatenate([s1, s2], axis=1))
    a, rank1 = vals[:, :tl], rank[:, :tl]
    b, rank2 = vals[:, tl:], rank[:, tl:]
    k = PEER_TOPK
    ea = jnp.exp(a - a[0:1])
    eb = jnp.exp(b - b[0:1])

    half = k // 2
    i_k = lax.broadcasted_iota(jnp.int32, (k, tl), 0).astype(F32)
    i_h = lax.broadcasted_iota(jnp.int32, (half, tl), 0).astype(F32)
    cand = [a[0:1] + b]
    prod = [ea[0:1] * eb]
    pos = [i_k]
    for r1 in range(1, half):
        cand.append(a[r1:r1 + 1] + b[0:half])
        prod.append(ea[r1:r1 + 1] * eb[0:half])
        pos.append(i_h + float(r1 * k))
    cand.append(a[half:k] + b[0:1])
    prod.append(ea[half:k] * eb[0:1])
    pos.append((i_h + float(half)) * float(k))
    cand = jnp.concatenate(cand, axis=0)
    prod = jnp.concatenate(prod, axis=0)
    pos = jnp.concatenate(pos, axis=0)
    picked = jnp.zeros_like(cand)
    for _ in range(k):
        m = jnp.max(cand, axis=0, keepdims=True)
        first = jnp.min(jnp.where(cand == m, pos, float(k * k)), axis=0, keepdims=True)
        hit = pos == first
        picked = jnp.where(hit, 1.0, picked)
        cand = jnp.where(hit, -jnp.inf, cand)
    z = jnp.sum(picked * prod, axis=0, keepdims=True)

    counts = [jnp.sum(picked[0:k], axis=0, keepdims=True)]
    for r1 in range(1, half):
        lo = k + (r1 - 1) * half
        counts.append(jnp.sum(picked[lo:lo + half], axis=0, keepdims=True))
    lo = k + (half - 1) * half
    for r in range(half):
        counts.append(picked[lo + r:lo + r + 1])
    n = jnp.zeros_like(s1)
    for r1 in range(k):
        n = jnp.where(rank1 == float(r1), counts[r1], n)

    return rank2, jnp.exp(s2 - b[0:1]), n, jnp.exp(s1 - a[0:1]) / z


def _route_kernel(x_ref, wq_ref, keys_ref, r2_ref, e2_ref, n_ref, c_ref, q_scr, *, hs):
    step = pl.program_id(1)
    ngrp = q_scr.shape[0]
    dk = q_scr.shape[2]

    @pl.when(step == 0)
    def _():
        q = jnp.dot(x_ref[...], wq_ref[...], preferred_element_type=F32)
        for g in range(ngrp):
            q_scr[g] = q[:, g * dk:(g + 1) * dk].astype(q_scr.dtype)

    tl = x_ref.shape[0]
    s1 = jnp.concatenate([lax.dot_general(keys_ref[2 * (step * hs + hh)], q_scr[2 * (step * hs + hh)], _NT,
                                          preferred_element_type=F32) for hh in range(hs)], axis=1)
    s2 = jnp.concatenate([lax.dot_general(keys_ref[2 * (step * hs + hh) + 1], q_scr[2 * (step * hs + hh) + 1],
                                          _NT, preferred_element_type=F32) for hh in range(hs)], axis=1)
    rank2, e2, n, c = _route_head(s1, s2)
    for hh in range(hs):
        sl = slice(hh * tl, (hh + 1) * tl)
        r2_ref[hh] = rank2[:, sl].astype(r2_ref.dtype)
        e2_ref[hh] = e2[:, sl].astype(e2_ref.dtype)
        n_ref[hh] = n[:, sl]
        c_ref[hh] = c[:, sl]


def _route(x1b, wq, keys):
    t, d = x1b.shape
    nq = wq.shape[1]
    ngrp = 2 * PEER_HEADS
    assert nq == ngrp * PEER_DHALF and PEER_TOPK == 16 and PEER_NKEYS % 8 == 0
    tl = _tile(t, 256)
    hs = min(2, PEER_HEADS)
    shp = jax.ShapeDtypeStruct((PEER_HEADS, PEER_NKEYS, t), F32)
    shp_b = jax.ShapeDtypeStruct((PEER_HEADS, PEER_NKEYS, t), BF16)
    ospec = pl.BlockSpec((hs, PEER_NKEYS, tl), lambda i, h: (h, 0, i))
    vmem = (2 * tl * d * 2 + d * nq * 2 + 2 * ngrp * PEER_NKEYS * PEER_DHALF * 2
            + 8 * PEER_NKEYS * tl * 4 + ngrp * tl * PEER_DHALF * 2 + tl * nq * 4 + (16 << 20))
    return pl.pallas_call(
        functools.partial(_route_kernel, hs=hs),
        out_shape=(shp_b, shp_b, shp, shp),
        grid=(t // tl, PEER_HEADS // hs),
        in_specs=[
            pl.BlockSpec((tl, d), lambda i, h: (i, 0)),
            pl.BlockSpec((d, nq), lambda i, h: (0, 0), pipeline_mode=pl.Buffered(1)),
            pl.BlockSpec((ngrp, PEER_NKEYS, PEER_DHALF), lambda i, h: (0, 0, 0)),
        ],
        out_specs=(ospec, ospec, ospec, ospec),
        scratch_shapes=[pltpu.VMEM((ngrp, tl, PEER_DHALF), BF16)],
        compiler_params=_params(("parallel", "arbitrary"), vmem),
        name="route",
    )(x1b, wq, keys)


def _peer_kernel(x_ref, u_ref, vt_ref, r2_ref, e2_ref, n_ref, c_ref, yt_ref, g_ref, *, rows):
    j = pl.program_id(1)
    nk = PEER_NKEYS

    @pl.when(j == 0)
    def _():
        yt_ref[...] = jnp.zeros_like(yt_ref)

    for a in range(rows):
        i1 = j * rows + a
        g = None
        for h in range(PEER_HEADS):
            nrow = n_ref[h, pl.ds(i1, 1), :].astype(BF16)
            crow = c_ref[h, pl.ds(i1, 1), :].astype(BF16)
            term = jnp.where(r2_ref[h] < nrow, e2_ref[h] * crow, jnp.zeros((), BF16))
            g = term if g is None else g + term
        g_ref[a * nk:(a + 1) * nk, :] = g

    te = rows * nk
    hte = te // 2
    x = x_ref[...]
    acc = None
    for s in range(2):
        sl = slice(s * hte, (s + 1) * hte)
        ht = lax.dot_general(u_ref[sl, :], x, _NT, preferred_element_type=F32)
        gelu = ht * (lax.erf(ht * (2.0 ** -0.5)) + 1.0) * 0.5
        act_t = gelu.astype(BF16) * g_ref[sl, :]
        part = jnp.dot(vt_ref[:, sl], act_t, preferred_element_type=F32)
        acc = part if acc is None else acc + part
    yt_ref[...] += acc


def _peer_dense(x1b, u, vt, r2, e2, nn, cc):
    t, d = x1b.shape
    ne = u.shape[0]
    assert ne == PEER_NKEYS * PEER_NKEYS and vt.shape == (d, ne)
    tm = _tile(t, 512)
    rows = 4
    te = rows * PEER_NKEYS
    hk = PEER_HEADS * PEER_NKEYS
    bspec = pl.BlockSpec((PEER_HEADS, PEER_NKEYS, tm), lambda i, j: (0, 0, i),
                         pipeline_mode=pl.Buffered(1))
    vmem = (tm * d * 2 + 4 * te * d * 2 + 2 * hk * tm * 2 + 2 * hk * tm * 4 + 2 * d * tm * 4
            + te * tm * 2 + 3 * te * tm * 4 + (4 << 20))
    return pl.pallas_call(
        functools.partial(_peer_kernel, rows=rows),
        out_shape=jax.ShapeDtypeStruct((d, t), F32),
        grid=(t // tm, ne // te),
        in_specs=[
            pl.BlockSpec((tm, d), lambda i, j: (i, 0), pipeline_mode=pl.Buffered(1)),
            pl.BlockSpec((te, d), lambda i, j: (j, 0)),
            pl.BlockSpec((d, te), lambda i, j: (0, j)),
            bspec, bspec, bspec, bspec,
        ],
        out_specs=pl.BlockSpec((d, tm), lambda i, j: (0, i)),
        scratch_shapes=[pltpu.VMEM((te, tm), BF16)],
        compiler_params=_params(("parallel", "arbitrary"), vmem),
        name="peer",
    )(x1b, u, vt, r2, e2, nn, cc)


def _ln2_kernel(x_ref, yt_ref, w_ref, b_ref, o_ref):
    z = DEEPNORM_ALPHA * x_ref[...] + yt_ref[...].T
    mu = jnp.mean(z, axis=-1, keepdims=True)
    d = z - mu
    var = jnp.mean(d * d, axis=-1, keepdims=True)
    o_ref[...] = (d * lax.rsqrt(var + NORM_EPS) * w_ref[...] + b_ref[...]).astype(o_ref.dtype)


def _ln2(x1f, yt, ln_w, ln_b):
    t, d = x1f.shape
    tm = _tile(t, 256)
    vmem = 2 * 3 * tm * d * 4 + 3 * tm * d * 4 + (4 << 20)
    return pl.pallas_call(
        _ln2_kernel,
        out_shape=jax.ShapeDtypeStruct((t, d), F32),
        grid=(t // tm,),
        in_specs=[
            pl.BlockSpec((tm, d), lambda i: (i, 0)),
            pl.BlockSpec((d, tm), lambda i: (0, i)),
            pl.BlockSpec((1, d), lambda i: (0, 0)),
            pl.BlockSpec((1, d), lambda i: (0, 0)),
        ],
        out_specs=pl.BlockSpec((tm, d), lambda i: (i, 0)),
        compiler_params=_params(("parallel",), vmem),
        name="ln2",
    )(x1f, yt, ln_w.astype(F32).reshape(1, d), ln_b.astype(F32).reshape(1, d))


def kernel(x, w_in, da_lambda, da_subln_w, w_da_out, ret_gn_w, ret_gn_b, w_ret_out, w_out,
           ln1_w, ln1_b, peer_w_query, peer_sub_keys, peer_u, peer_v, ln2_w, ln2_b):
    bsz, seq, d = x.shape
    t = bsz * seq
    da_qk_w, da_v_w = DA_HEADS * 2 * DA_DH, DA_HEADS * DA_DV
    ret_qk_w, ret_v_w = RET_HEADS * RET_DK, RET_HEADS * RET_DV
    splits = (da_qk_w, da_qk_w, da_v_w, ret_qk_w, ret_qk_w, ret_v_w, ret_v_w, d, d)
    offs = [0]
    for s in splits:
        offs.append(offs[-1] + s)
    o_q, o_k, o_v, o_rq, o_rk, o_rv, o_rg, o_ga, o_gr = offs[:-1]

    xf = x.reshape(t, d)
    for l in range(DEPTH):
        xb = xf.astype(BF16)
        proj = _matmul(xb, w_in[l].astype(BF16), BF16, "proj")
        a = _diff_attention(proj, da_lambda[l], da_subln_w[l], bsz, seq, o_q, o_k, o_v, l)
        r = _retention(proj, ret_gn_w[l], ret_gn_b[l], bsz, seq, o_rq, o_rk, o_rv, o_rg)
        merged = _merge(a, w_da_out[l].astype(BF16), r, w_ret_out[l].astype(BF16), proj, o_ga, o_gr, d)
        x1f, x1b = _out_ln(merged, w_out[l].astype(BF16), xf, ln1_w[l], ln1_b[l])
        keys = peer_sub_keys[l].astype(BF16).reshape(2 * PEER_HEADS, PEER_NKEYS, PEER_DHALF)
        r2, e2, nn, cc = _route(x1b, peer_w_query[l].astype(BF16), keys)
        yt = _peer_dense(x1b, peer_u[l].astype(BF16), peer_v[l].astype(BF16).T, r2, e2, nn, cc)
        xf = _ln2(x1f, yt, ln2_w[l], ln2_b[l])
    return xf.reshape(bsz, seq, d)
```

```python
import functools
import math

import jax
import jax.numpy as jnp
from jax import lax
from jax.experimental import pallas as pl
from jax.experimental.pallas import tpu as pltpu

DEPTH = 1
DA_HEADS = 16
DA_DH = 64
DA_DV = 2 * DA_DH
RET_HEADS = 16
RET_DK = 128
RET_DV = 256
RET_CHUNK = 128
PEER_HEADS = 8
PEER_NKEYS = 128
PEER_DHALF = 128
PEER_TOPK = 16
NORM_EPS = 1e-5
DEEPNORM_ALPHA = (2.0 * DEPTH) ** 0.25

V7X_VMEM_BYTES = 64 * 1024 * 1024
LANES = 128
SOFTMAX_ROWS = 128
F32 = jnp.float32
BF16 = jnp.bfloat16

_NT = (((1,), (1,)), ((), ()))
_TN = (((0,), (0,)), ((), ()))


def _tile(n, pref):
    if n <= pref:
        return n
    t = pref - pref % LANES
    while t >= LANES:
        if n % t == 0:
            return t
        t -= LANES
    raise ValueError(f"no tile for {n} <= {pref}")


def _params(sem, vmem_bytes, flags=None):
    limit = min(int(vmem_bytes), V7X_VMEM_BYTES - 4 * 1024 * 1024)
    return pltpu.CompilerParams(dimension_semantics=sem, vmem_limit_bytes=limit, flags=flags)


def _sigmoid(x):
    return 1.0 / (1.0 + jnp.exp(-x))


def _mm_kernel(a_ref, b_ref, o_ref):
    o_ref[...] = jnp.dot(a_ref[...], b_ref[...], preferred_element_type=F32).astype(o_ref.dtype)


def _matmul(a, b, out_dtype, name, tm_pref=1024, tn_pref=1024):
    m, k = a.shape
    n = b.shape[1]
    tm, tn = _tile(m, tm_pref), _tile(n, tn_pref)
    osz = jnp.dtype(out_dtype).itemsize
    vmem = 2 * (tm * k * 2 + k * tn * 2 + tm * tn * osz) + tm * tn * 4 + (4 << 20)
    return pl.pallas_call(
        _mm_kernel,
        out_shape=jax.ShapeDtypeStruct((m, n), out_dtype),
        grid=(m // tm, n // tn),
        in_specs=[pl.BlockSpec((tm, k), lambda i, j: (i, 0)),
                  pl.BlockSpec((k, tn), lambda i, j: (0, j))],
        out_specs=pl.BlockSpec((tm, tn), lambda i, j: (i, j)),
        compiler_params=_params(("parallel", "parallel"), vmem),
        name=name,
    )(a, b)


def _da_kernel(parts_ref, lam_ref, q_ref, k_ref, v_ref, w_ref, o_ref, s_ref, p_ref, kaug_ref, l_ref, *,
               tq, hb, nq, lam_init):
    hg = pl.program_id(1)
    qi = pl.program_id(2)
    lp = lam_ref[...]
    lam = (jnp.exp(jnp.sum(lp[0:1] * lp[1:2], axis=-1, keepdims=True))
           - jnp.exp(jnp.sum(lp[2:3] * lp[3:4], axis=-1, keepdims=True)) + lam_init)
    hw = q_ref.shape[1] // hb
    dh = hw // 2
    dv = v_ref.shape[1] // hb
    seq = k_ref.shape[0]
    lane = lax.broadcasted_iota(jnp.int32, (tq, hw), 1)

    def aug_columns(pos, lane_idx, h, key_side):
        hi = (pos >> 8).astype(F32)
        lo = (pos & 255).astype(F32)
        cols = jnp.zeros(pos.shape, F32)
        for i in range(3):
            part = parts_ref[3 * h + i]
            const = (-256.0 * part, -part) if key_side else (256.0 * part, part)
            var = (6 + i, 9 + i) if key_side else (i, 3 + i)
            fixed = (i, 3 + i) if key_side else (6 + i, 9 + i)
            cols = jnp.where(lane_idx == fixed[0], const[0], cols)
            cols = jnp.where(lane_idx == fixed[1], const[1], cols)
            cols = jnp.where(lane_idx == var[0], hi, cols)
            cols = jnp.where(lane_idx == var[1], lo, cols)
        return cols.astype(BF16)

    @pl.when(qi == 0)
    def _():
        kpos = lax.broadcasted_iota(jnp.int32, (seq, LANES), 0)
        klane = lax.broadcasted_iota(jnp.int32, (seq, LANES), 1)
        for hh in range(hb):
            kaug_ref[hh] = aug_columns(kpos, klane, hg * hb + hh, True)

    def run(nblk):
        nk = nblk * tq
        qpos = lax.broadcasted_iota(jnp.int32, (tq, LANES), 0) + (nk - tq)
        qlane = lax.broadcasted_iota(jnp.int32, (tq, LANES), 1)
        row = lax.broadcasted_iota(jnp.int32, (tq, tq), 0)
        col = lax.broadcasted_iota(jnp.int32, (tq, tq), 1)
        future = jnp.concatenate([col > row, col > row], axis=0)
        for hh in range(hb):
            q = q_ref[:, hh * hw:(hh + 1) * hw]
            qs = (q.astype(F32) * (dh ** -0.5)).astype(BF16)
            zero = jnp.zeros_like(qs)
            qaug = aug_columns(qpos, qlane, hg * hb + hh, False)
            qc = jnp.concatenate(
                [jnp.concatenate([jnp.where(lane < dh, qs, zero), qaug], axis=1),
                 jnp.concatenate([jnp.where(lane >= dh, qs, zero), qaug], axis=1)], axis=0)
            k = jnp.concatenate([k_ref[0:nk, hh * hw:(hh + 1) * hw], kaug_ref[hh, 0:nk, :]], axis=1)
            s = lax.dot_general(qc, k, _NT, preferred_element_type=F32)
            last = jnp.where(future, -jnp.inf, s[:, nk - tq:])
            s = last if nblk == 1 else jnp.concatenate([s[:, :nk - tq], last], axis=1)
            s_ref[hh, :, 0:nk] = s

            for r0 in range(0, 2 * tq, SOFTMAX_ROWS):
                sc = s_ref[hh, r0:r0 + SOFTMAX_ROWS, 0:nk]
                p = jnp.exp(sc - jnp.max(sc, axis=-1, keepdims=True))
                l_ref[hh, r0:r0 + SOFTMAX_ROWS, :] = jnp.sum(p, axis=-1, keepdims=True)
                p_ref[hh, r0:r0 + SOFTMAX_ROWS, 0:nk] = p.astype(BF16)
            l = l_ref[hh]
            pv = jnp.dot(p_ref[hh, :, 0:nk], v_ref[0:nk, hh * dv:(hh + 1) * dv],
                         preferred_element_type=F32)
            o = pv[:tq] / l[:tq] - lam * (pv[tq:] / l[tq:])
            ms = jnp.mean(o * o, axis=-1, keepdims=True)
            o = o * (lax.rsqrt(ms + NORM_EPS) * (1.0 - lam_init)) * w_ref[...]
            o_ref[:, hh * dv:(hh + 1) * dv] = o.astype(o_ref.dtype)

    lax.switch(qi, [functools.partial(run, n) for n in range(1, nq + 1)])


def _diff_attention(proj, da_lambda, subln_w, bsz, seq, q_off, k_off, v_off, layer_idx):
    t = bsz * seq
    hw = 2 * DA_DH
    hb = min(2, DA_HEADS)
    ng = DA_HEADS // hb
    gw = hb * hw
    assert hw == DA_DV and q_off % gw == 0 and k_off % gw == 0 and v_off % gw == 0
    tq = _tile(seq, 256)
    nq = seq // tq
    lam_init = 0.8 - 0.6 * math.exp(-0.3 * layer_idx)
    slopes = 2.0 ** (-8.0 * jnp.arange(1, DA_HEADS + 1, dtype=F32) / DA_HEADS)
    p1 = slopes.astype(BF16).astype(F32)
    p2 = (slopes - p1).astype(BF16).astype(F32)
    p3 = (slopes - p1 - p2).astype(BF16).astype(F32)
    parts = jnp.stack([p1, p2, p3], axis=1).reshape(-1)
    qb, kb, vb = q_off // gw, k_off // gw, v_off // gw
    vmem = (2 * (2 * tq * gw * 2 + 2 * seq * gw * 2) + hb * 2 * tq * seq * 6 + hb * seq * LANES * 2
            + 3 * 2 * tq * seq * 4 + (8 << 20))
    return pl.pallas_call(
        functools.partial(_da_kernel, tq=tq, hb=hb, nq=nq, lam_init=lam_init),
        out_shape=jax.ShapeDtypeStruct((t, DA_HEADS * DA_DV), BF16),
        grid=(bsz, ng, nq),
        in_specs=[
            pl.BlockSpec(memory_space=pltpu.SMEM),
            pl.BlockSpec((4, DA_DH), lambda b, g, i: (0, 0)),
            pl.BlockSpec((tq, gw), lambda b, g, i: (b * nq + i, qb + g)),
            pl.BlockSpec((seq, gw), lambda b, g, i: (b, kb + g)),
            pl.BlockSpec((seq, gw), lambda b, g, i: (b, vb + g)),
            pl.BlockSpec((1, DA_DV), lambda b, g, i: (0, 0)),
        ],
        out_specs=pl.BlockSpec((tq, gw), lambda b, g, i: (b * nq + i, g)),
        scratch_shapes=[pltpu.VMEM((hb, 2 * tq, seq), F32), pltpu.VMEM((hb, 2 * tq, seq), BF16),
                        pltpu.VMEM((hb, seq, LANES), BF16), pltpu.VMEM((hb, 2 * tq, 1), F32)],
        compiler_params=_params(("parallel", "parallel", "arbitrary"), vmem),
        name="diffattn",
    )(parts, da_lambda.astype(F32), proj, proj, proj, subln_w.astype(F32).reshape(1, DA_DV))


def _ret_kernel(cd_ref, q_ref, k_ref, v_ref, g_ref, dm_ref, cross_ref, sd_ref, w_ref, b_ref,
                o_ref, state_ref, *, hg):
    grp = pl.program_id(1)
    c = pl.program_id(2)

    @pl.when(c == 0)
    def _():
        state_ref[...] = jnp.zeros_like(state_ref)

    for hh in range(hg):
        qh = q_ref[:, hh * RET_DK:(hh + 1) * RET_DK]
        kh = k_ref[:, hh * RET_DK:(hh + 1) * RET_DK]
        vh = v_ref[:, hh * RET_DV:(hh + 1) * RET_DV]
        st = state_ref[hh]
        inner = lax.dot_general(qh, kh, _NT, preferred_element_type=F32) * dm_ref[hh]
        out = (jnp.dot(inner.astype(BF16), vh, preferred_element_type=F32)
               + jnp.dot(qh, st.astype(BF16), preferred_element_type=F32) * cross_ref[hh])
        ks = (kh.astype(F32) * sd_ref[hh]).astype(BF16)
        state_ref[hh] = (st * cd_ref[grp * hg + hh]
                         + lax.dot_general(ks, vh, _TN, preferred_element_type=F32))
        mu = jnp.mean(out, axis=-1, keepdims=True)
        d = out - mu
        var = jnp.mean(d * d, axis=-1, keepdims=True)
        sl = slice(hh * RET_DV, (hh + 1) * RET_DV)
        y = d * lax.rsqrt(var + NORM_EPS) * w_ref[:, sl] + b_ref[:, sl]
        g = g_ref[:, sl].astype(F32)
        o_ref[:, sl] = (y * (g * _sigmoid(g))).astype(o_ref.dtype)


def _retention(proj, gn_w, gn_b, bsz, seq, q_off, k_off, v_off, g_off):
    t = bsz * seq
    nc = seq // RET_CHUNK
    hg = min(8, RET_HEADS)
    ngrp = RET_HEADS // hg
    qw, vw = hg * RET_DK, hg * RET_DV
    assert q_off % qw == 0 and k_off % qw == 0 and v_off % vw == 0 and g_off % vw == 0
    log_g = jnp.log(1.0 - 2.0 ** (-5.0 - jnp.arange(RET_HEADS, dtype=F32)))
    idx = jnp.arange(RET_CHUNK, dtype=F32)
    rel = idx[:, None] - idx[None, :]
    scale = RET_DK ** -0.5
    dmask = jnp.where(rel >= 0, jnp.exp(log_g[:, None, None] * jnp.maximum(rel, 0.0)), 0.0) * scale
    cross = jnp.exp(log_g[:, None] * (idx + 1.0))[:, :, None]
    sdec = (jnp.exp(log_g[:, None] * (RET_CHUNK - 1.0 - idx)) * scale)[:, :, None]
    cdec = jnp.exp(log_g * RET_CHUNK)
    qb, kb, vb, gb = q_off // qw, k_off // qw, v_off // vw, g_off // vw
    c_ = RET_CHUNK
    vmem = (2 * (2 * c_ * qw * 2 + 3 * c_ * vw * 2 + hg * c_ * c_ * 4 + 2 * hg * c_ * LANES * 4)
            + hg * RET_DK * RET_DV * 4 + (16 << 20))
    return pl.pallas_call(
        functools.partial(_ret_kernel, hg=hg),
        out_shape=jax.ShapeDtypeStruct((t, RET_HEADS * RET_DV), BF16),
        grid=(bsz, ngrp, nc),
        in_specs=[
            pl.BlockSpec(memory_space=pltpu.SMEM),
            pl.BlockSpec((c_, qw), lambda b, g, c: (b * nc + c, qb + g)),
            pl.BlockSpec((c_, qw), lambda b, g, c: (b * nc + c, kb + g)),
            pl.BlockSpec((c_, vw), lambda b, g, c: (b * nc + c, vb + g)),
            pl.BlockSpec((c_, vw), lambda b, g, c: (b * nc + c, gb + g)),
            pl.BlockSpec((hg, c_, c_), lambda b, g, c: (g, 0, 0)),
            pl.BlockSpec((hg, c_, 1), lambda b, g, c: (g, 0, 0)),
            pl.BlockSpec((hg, c_, 1), lambda b, g, c: (g, 0, 0)),
            pl.BlockSpec((1, vw), lambda b, g, c: (0, g)),
            pl.BlockSpec((1, vw), lambda b, g, c: (0, g)),
        ],
        out_specs=pl.BlockSpec((c_, vw), lambda b, g, c: (b * nc + c, g)),
        scratch_shapes=[pltpu.VMEM((hg, RET_DK, RET_DV), F32)],
        compiler_params=_params(("parallel", "parallel", "arbitrary"), vmem),
        name="retention",
    )(cdec, proj, proj, proj, proj, dmask, cross, sdec,
      gn_w.astype(F32).reshape(1, -1), gn_b.astype(F32).reshape(1, -1))


def _merge_kernel(a_ref, wd_ref, r_ref, wr_ref, ga_ref, gr_ref, o_ref):
    a2 = jnp.dot(a_ref[...], wd_ref[...], preferred_element_type=F32)
    r2 = jnp.dot(r_ref[...], wr_ref[...], preferred_element_type=F32)
    ga = ga_ref[...].astype(F32)
    gr = gr_ref[...].astype(F32)
    o_ref[...] = (_sigmoid(ga) * a2 + _sigmoid(gr) * r2).astype(o_ref.dtype)


def _merge(a, wd, r, wr, proj, ga_off, gr_off, d_model):
    t = a.shape[0]
    ka, kr = a.shape[1], r.shape[1]
    tm, tn = _tile(t, 512), _tile(d_model, 1024)
    assert ga_off % tn == 0 and gr_off % tn == 0
    gab, grb = ga_off // tn, gr_off // tn
    vmem = 2 * ((tm + tn) * (ka + kr) * 2 + 3 * tm * tn * 2) + 3 * tm * tn * 4 + (4 << 20)
    return pl.pallas_call(
        _merge_kernel,
        out_shape=jax.ShapeDtypeStruct((t, d_model), BF16),
        grid=(t // tm, d_model // tn),
        in_specs=[
            pl.BlockSpec((tm, ka), lambda i, j: (i, 0)),
            pl.BlockSpec((ka, tn), lambda i, j: (0, j)),
            pl.BlockSpec((tm, kr), lambda i, j: (i, 0)),
            pl.BlockSpec((kr, tn), lambda i, j: (0, j)),
            pl.BlockSpec((tm, tn), lambda i, j: (i, gab + j)),
            pl.BlockSpec((tm, tn), lambda i, j: (i, grb + j)),
        ],
        out_specs=pl.BlockSpec((tm, tn), lambda i, j: (i, j)),
        compiler_params=_params(("parallel", "parallel"), vmem),
        name="merge",
    )(a, wd, r, wr, proj, proj)


def _outln_kernel(m_ref, w_ref, x_ref, lw_ref, lb_ref, of_ref, ob_ref, *, nj, tn):
    j = pl.program_id(1)
    z = DEEPNORM_ALPHA * x_ref[...] + jnp.dot(m_ref[...], w_ref[...], preferred_element_type=F32)

    def store(jj):
        def f():
            of_ref[:, jj * tn:(jj + 1) * tn] = z
        return f

    lax.switch(j, [store(jj) for jj in range(nj)])

    @pl.when(j == nj - 1)
    def _():
        d = nj * tn
        cols = [slice(jj * tn, (jj + 1) * tn) for jj in range(nj)]
        tot = of_ref[:, cols[0]].sum(axis=-1, keepdims=True)
        for sl in cols[1:]:
            tot = tot + of_ref[:, sl].sum(axis=-1, keepdims=True)
        mu = tot / d
        sq = jnp.square(of_ref[:, cols[0]] - mu).sum(axis=-1, keepdims=True)
        for sl in cols[1:]:
            sq = sq + jnp.square(of_ref[:, sl] - mu).sum(axis=-1, keepdims=True)
        rstd = lax.rsqrt(sq / d + NORM_EPS)
        for sl in cols:
            y = (of_ref[:, sl] - mu) * rstd * lw_ref[:, sl] + lb_ref[:, sl]
            of_ref[:, sl] = y
            ob_ref[:, sl] = y.astype(ob_ref.dtype)


def _out_ln(merged, w_out, x2d, ln_w, ln_b):
    t, d = x2d.shape
    tm, tn = _tile(t, 512), _tile(d, 512)
    nj = d // tn
    vmem = (2 * (tm * d * 2 + d * tn * 2 + tm * tn * 4 + tm * d * 4 + tm * d * 2)
            + 3 * tm * tn * 4 + (4 << 20))
    return pl.pallas_call(
        functools.partial(_outln_kernel, nj=nj, tn=tn),
        out_shape=(jax.ShapeDtypeStruct((t, d), F32), jax.ShapeDtypeStruct((t, d), BF16)),
        grid=(t // tm, nj),
        in_specs=[
            pl.BlockSpec((tm, d), lambda i, j: (i, 0)),
            pl.BlockSpec((d, tn), lambda i, j: (0, j)),
            pl.BlockSpec((tm, tn), lambda i, j: (i, j)),
            pl.BlockSpec((1, d), lambda i, j: (0, 0)),
            pl.BlockSpec((1, d), lambda i, j: (0, 0)),
        ],
        out_specs=(pl.BlockSpec((tm, d), lambda i, j: (i, 0)),
                   pl.BlockSpec((tm, d), lambda i, j: (i, 0))),
        compiler_params=_params(("parallel", "arbitrary"), vmem),
        name="outln",
    )(merged, w_out, x2d, ln_w.astype(F32).reshape(1, d), ln_b.astype(F32).reshape(1, d))


_NOT_TOP = 99.0


def _top16(s, exact):
    nk, tl = s.shape
    key = lax.broadcasted_iota(jnp.int32, (nk, tl), 0).astype(F32)
    rank = jnp.full((nk, tl), _NOT_TOP, F32)
    vals = []
    for r in range(PEER_TOPK):
        m = jnp.max(s, axis=0, keepdims=True)
        if exact:
            first = jnp.min(jnp.where(s == m, key, float(nk)), axis=0, keepdims=True)
            hit = key == first
        else:
            hit = s == m
        rank = jnp.where(hit, float(r), rank)
        s = jnp.where(hit, -jnp.inf, s)
        vals.append(m)
    return jnp.concatenate(vals, axis=0), rank


def _route_head(s1, s2, exact):
    tl = s1.shape[1]
    vals, rank = _top16(jnp.concatenate([s1, s2], axis=1), exact)
    a, rank1 = vals[:, :tl], rank[:, :tl]
    b, rank2 = vals[:, tl:], rank[:, tl:]
    k = PEER_TOPK
    ea = jnp.exp(a - a[0:1])
    eb = jnp.exp(b - b[0:1])

    half = k // 2
    i_k = lax.broadcasted_iota(jnp.int32, (k, tl), 0).astype(F32)
    i_h = lax.broadcasted_iota(jnp.int32, (half, tl), 0).astype(F32)
    cand = [a[0:1] + b]
    prod = [ea[0:1] * eb]
    pos = [i_k]
    for r1 in range(1, half):
        cand.append(a[r1:r1 + 1] + b[0:half])
        prod.append(ea[r1:r1 + 1] * eb[0:half])
        pos.append(i_h + float(r1 * k))
    cand.append(a[half:k] + b[0:1])
    prod.append(ea[half:k] * eb[0:1])
    pos.append((i_h + float(half)) * float(k))
    cand = jnp.concatenate(cand, axis=0)
    prod = jnp.concatenate(prod, axis=0)
    pos = jnp.concatenate(pos, axis=0)
    picked = jnp.zeros_like(cand)
    for _ in range(k):
        m = jnp.max(cand, axis=0, keepdims=True)
        if exact:
            first = jnp.min(jnp.where(cand == m, pos, float(k * k)), axis=0, keepdims=True)
            hit = pos == first
        else:
            hit = cand == m
        picked = jnp.where(hit, 1.0, picked)
        cand = jnp.where(hit, -jnp.inf, cand)
    z = jnp.sum(picked * prod, axis=0, keepdims=True)

    counts = [jnp.sum(picked[0:k], axis=0, keepdims=True)]
    for r1 in range(1, half):
        lo = k + (r1 - 1) * half
        counts.append(jnp.sum(picked[lo:lo + half], axis=0, keepdims=True))
    lo = k + (half - 1) * half
    for r in range(half):
        counts.append(picked[lo + r:lo + r + 1])
    n = jnp.zeros_like(s1)
    for r1 in range(k):
        n = jnp.where(rank1 == float(r1), counts[r1], n)

    ranked = jnp.sum(jnp.where(rank != _NOT_TOP, 1.0, 0.0), axis=0, keepdims=True)
    chosen = ranked[:, :tl] + ranked[:, tl:] + jnp.sum(picked, axis=0, keepdims=True)
    return rank2, jnp.exp(s2 - b[0:1]), n, jnp.exp(s1 - a[0:1]) / z, chosen


def _route_kernel(x_ref, wq_ref, keys_ref, r2_ref, e2_ref, n_ref, c_ref, q_scr, *, hs):
    step = pl.program_id(1)
    ngrp = q_scr.shape[0]
    dk = q_scr.shape[2]

    @pl.when(step == 0)
    def _():
        q = jnp.dot(x_ref[...], wq_ref[...], preferred_element_type=F32)
        for g in range(ngrp):
            q_scr[g] = q[:, g * dk:(g + 1) * dk].astype(q_scr.dtype)

    tl = x_ref.shape[0]
    s1 = jnp.concatenate([lax.dot_general(keys_ref[2 * (step * hs + hh)], q_scr[2 * (step * hs + hh)], _NT,
                                          preferred_element_type=F32) for hh in range(hs)], axis=1)
    s2 = jnp.concatenate([lax.dot_general(keys_ref[2 * (step * hs + hh) + 1], q_scr[2 * (step * hs + hh) + 1],
                                          _NT, preferred_element_type=F32) for hh in range(hs)], axis=1)

    def write(rank2, e2, n, c):
        for hh in range(hs):
            sl = slice(hh * tl, (hh + 1) * tl)
            r2_ref[hh] = rank2[:, sl].astype(r2_ref.dtype)
            e2_ref[hh] = e2[:, sl].astype(e2_ref.dtype)
            n_ref[hh] = n[:, sl]
            c_ref[hh] = c[:, sl]

    fast = _route_head(s1, s2, exact=False)
    no_ties = jnp.max(fast[4]) == 3.0 * PEER_TOPK
    lax.cond(no_ties, lambda: write(*fast[:4]), lambda: write(*_route_head(s1, s2, exact=True)[:4]))


def _route(x1b, wq, keys):
    t, d = x1b.shape
    nq = wq.shape[1]
    ngrp = 2 * PEER_HEADS
    assert nq == ngrp * PEER_DHALF and PEER_TOPK == 16 and PEER_NKEYS % 8 == 0
    tl = _tile(t, 256)
    hs = min(2, PEER_HEADS)
    shp = jax.ShapeDtypeStruct((PEER_HEADS, PEER_NKEYS, t), F32)
    shp_b = jax.ShapeDtypeStruct((PEER_HEADS, PEER_NKEYS, t), BF16)
    ospec = pl.BlockSpec((hs, PEER_NKEYS, tl), lambda i, h: (h, 0, i))
    vmem = (2 * tl * d * 2 + d * nq * 2 + 2 * ngrp * PEER_NKEYS * PEER_DHALF * 2
            + 8 * PEER_NKEYS * tl * 4 + ngrp * tl * PEER_DHALF * 2 + tl * nq * 4 + (16 << 20))
    return pl.pallas_call(
        functools.partial(_route_kernel, hs=hs),
        out_shape=(shp_b, shp_b, shp, shp),
        grid=(t // tl, PEER_HEADS // hs),
        in_specs=[
            pl.BlockSpec((tl, d), lambda i, h: (i, 0)),
            pl.BlockSpec((d, nq), lambda i, h: (0, 0), pipeline_mode=pl.Buffered(1)),
            pl.BlockSpec((ngrp, PEER_NKEYS, PEER_DHALF), lambda i, h: (0, 0, 0)),
        ],
        out_specs=(ospec, ospec, ospec, ospec),
        scratch_shapes=[pltpu.VMEM((ngrp, tl, PEER_DHALF), BF16)],
        compiler_params=_params(("parallel", "arbitrary"), vmem),
        name="route",
    )(x1b, wq, keys)


def _vt_kernel(v_ref, o_ref):
    o_ref[...] = v_ref[...].T.astype(o_ref.dtype)


def _transpose_to_bf16(v):
    ne, d = v.shape
    te = _tile(ne, 256)
    vmem = 2 * (te * d * 4 + d * te * 2) + 3 * te * d * 4 + (2 << 20)
    return pl.pallas_call(
        _vt_kernel,
        out_shape=jax.ShapeDtypeStruct((d, ne), BF16),
        grid=(ne // te,),
        in_specs=[pl.BlockSpec((te, d), lambda i: (i, 0))],
        out_specs=pl.BlockSpec((d, te), lambda i: (0, i)),
        compiler_params=_params(("parallel",), vmem),
        name="vtranspose",
    )(v)


def _peer_kernel(x_ref, u_ref, vt_ref, r2_ref, e2_ref, n_ref, c_ref, yt_ref, g_ref, *, rows):
    j = pl.program_id(1)
    nk = PEER_NKEYS

    @pl.when(j == 0)
    def _():
        yt_ref[...] = jnp.zeros_like(yt_ref)

    for a in range(rows):
        i1 = j * rows + a
        g = None
        for h in range(PEER_HEADS):
            nrow = n_ref[h, pl.ds(i1, 1), :].astype(BF16)
            crow = c_ref[h, pl.ds(i1, 1), :].astype(BF16)
            term = jnp.where(r2_ref[h] < nrow, e2_ref[h] * crow, jnp.zeros((), BF16))
            g = term if g is None else g + term
        g_ref[a * nk:(a + 1) * nk, :] = g

    te = rows * nk
    hte = te // 2
    x = x_ref[...]
    acc = None
    for s in range(2):
        sl = slice(s * hte, (s + 1) * hte)
        ht = lax.dot_general(u_ref[sl, :], x, _NT, preferred_element_type=F32)
        gelu = ht * (lax.erf(ht * (2.0 ** -0.5)) + 1.0) * 0.5
        act_t = gelu.astype(BF16) * g_ref[sl, :]
        part = jnp.dot(vt_ref[:, sl], act_t, preferred_element_type=F32)
        acc = part if acc is None else acc + part
    yt_ref[...] += acc


def _peer_dense(x1b, u, vt, r2, e2, nn, cc):
    t, d = x1b.shape
    ne = u.shape[0]
    assert ne == PEER_NKEYS * PEER_NKEYS and vt.shape == (d, ne)
    tm = _tile(t, 512)
    rows = 4
    te = rows * PEER_NKEYS
    hk = PEER_HEADS * PEER_NKEYS
    bspec = pl.BlockSpec((PEER_HEADS, PEER_NKEYS, tm), lambda i, j: (0, 0, i),
                         pipeline_mode=pl.Buffered(1))
    vmem = (tm * d * 2 + 4 * te * d * 2 + 2 * hk * tm * 2 + 2 * hk * tm * 4 + 2 * d * tm * 4
            + te * tm * 2 + 3 * te * tm * 4 + (4 << 20))
    return pl.pallas_call(
        functools.partial(_peer_kernel, rows=rows),
        out_shape=jax.ShapeDtypeStruct((d, t), F32),
        grid=(t // tm, ne // te),
        in_specs=[
            pl.BlockSpec((tm, d), lambda i, j: (i, 0), pipeline_mode=pl.Buffered(1)),
            pl.BlockSpec((te, d), lambda i, j: (j, 0)),
            pl.BlockSpec((d, te), lambda i, j: (0, j)),
            bspec, bspec, bspec, bspec,
        ],
        out_specs=pl.BlockSpec((d, tm), lambda i, j: (0, i)),
        scratch_shapes=[pltpu.VMEM((te, tm), BF16)],
        compiler_params=_params(("parallel", "arbitrary"), vmem),
        name="peer",
    )(x1b, u, vt, r2, e2, nn, cc)


def _ln2_kernel(x_ref, yt_ref, w_ref, b_ref, o_ref):
    z = DEEPNORM_ALPHA * x_ref[...] + yt_ref[...].T
    mu = jnp.mean(z, axis=-1, keepdims=True)
    d = z - mu
    var = jnp.mean(d * d, axis=-1, keepdims=True)
    o_ref[...] = (d * lax.rsqrt(var + NORM_EPS) * w_ref[...] + b_ref[...]).astype(o_ref.dtype)


def _ln2(x1f, yt, ln_w, ln_b):
    t, d = x1f.shape
    tm = _tile(t, 256)
    vmem = 2 * 3 * tm * d * 4 + 3 * tm * d * 4 + (4 << 20)
    return pl.pallas_call(
        _ln2_kernel,
        out_shape=jax.ShapeDtypeStruct((t, d), F32),
        grid=(t // tm,),
        in_specs=[
            pl.BlockSpec((tm, d), lambda i: (i, 0)),
            pl.BlockSpec((d, tm), lambda i: (0, i)),
            pl.BlockSpec((1, d), lambda i: (0, 0)),
            pl.BlockSpec((1, d), lambda i: (0, 0)),
        ],
        out_specs=pl.BlockSpec((tm, d), lambda i: (i, 0)),
        compiler_params=_params(("parallel",), vmem),
        name="ln2",
    )(x1f, yt, ln_w.astype(F32).reshape(1, d), ln_b.astype(F32).reshape(1, d))


def kernel(x, w_in, da_lambda, da_subln_w, w_da_out, ret_gn_w, ret_gn_b, w_ret_out, w_out,
           ln1_w, ln1_b, peer_w_query, peer_sub_keys, peer_u, peer_v, ln2_w, ln2_b):
    bsz, seq, d = x.shape
    t = bsz * seq
    da_qk_w, da_v_w = DA_HEADS * 2 * DA_DH, DA_HEADS * DA_DV
    ret_qk_w, ret_v_w = RET_HEADS * RET_DK, RET_HEADS * RET_DV
    splits = (da_qk_w, da_qk_w, da_v_w, ret_qk_w, ret_qk_w, ret_v_w, ret_v_w, d, d)
    offs = [0]
    for s in splits:
        offs.append(offs[-1] + s)
    o_q, o_k, o_v, o_rq, o_rk, o_rv, o_rg, o_ga, o_gr = offs[:-1]

    xf = x.reshape(t, d)
    for l in range(DEPTH):
        xb = xf.astype(BF16)
        proj = _matmul(xb, w_in[l].astype(BF16), BF16, "proj")
        a = _diff_attention(proj, da_lambda[l], da_subln_w[l], bsz, seq, o_q, o_k, o_v, l)
        r = _retention(proj, ret_gn_w[l], ret_gn_b[l], bsz, seq, o_rq, o_rk, o_rv, o_rg)
        merged = _merge(a, w_da_out[l].astype(BF16), r, w_ret_out[l].astype(BF16), proj, o_ga, o_gr, d)
        x1f, x1b = _out_ln(merged, w_out[l].astype(BF16), xf, ln1_w[l], ln1_b[l])
        keys = peer_sub_keys[l].astype(BF16).reshape(2 * PEER_HEADS, PEER_NKEYS, PEER_DHALF)
        r2, e2, nn, cc = _route(x1b, peer_w_query[l].astype(BF16), keys)
        yt = _peer_dense(x1b, peer_u[l].astype(BF16), _transpose_to_bf16(peer_v[l]), r2, e2, nn, cc)
        xf = _ln2(x1f, yt, ln2_w[l], ln2_b[l])
    return xf.reshape(bsz, seq, d)
```

```python
import functools
import math

import jax
import jax.numpy as jnp
from jax import lax
from jax.experimental import pallas as pl
from jax.experimental.pallas import tpu as pltpu

DEPTH = 1
DA_HEADS = 16
DA_DH = 64
DA_DV = 2 * DA_DH
RET_HEADS = 16
RET_DK = 128
RET_DV = 256
RET_CHUNK = 128
PEER_HEADS = 8
PEER_NKEYS = 128
PEER_DHALF = 128
PEER_TOPK = 16
NORM_EPS = 1e-5
DEEPNORM_ALPHA = (2.0 * DEPTH) ** 0.25

V7X_VMEM_BYTES = 64 * 1024 * 1024
LANES = 128
SOFTMAX_ROWS = 128
F32 = jnp.float32
BF16 = jnp.bfloat16

_NT = (((1,), (1,)), ((), ()))
_TN = (((0,), (0,)), ((), ()))


def _tile(n, pref):
    if n <= pref:
        return n
    t = pref - pref % LANES
    while t >= LANES:
        if n % t == 0:
            return t
        t -= LANES
    raise ValueError(f"no tile for {n} <= {pref}")


def _params(sem, vmem_bytes, flags=None):
    limit = min(int(vmem_bytes), V7X_VMEM_BYTES - 4 * 1024 * 1024)
    return pltpu.CompilerParams(dimension_semantics=sem, vmem_limit_bytes=limit, flags=flags)


def _sigmoid(x):
    return 0.5 * jnp.tanh(0.5 * x) + 0.5


def _mm_kernel(a_ref, b_ref, o_ref):
    o_ref[...] = jnp.dot(a_ref[...], b_ref[...], preferred_element_type=F32).astype(o_ref.dtype)


def _matmul(a, b, out_dtype, name, tm_pref=1024, tn_pref=1024):
    m, k = a.shape
    n = b.shape[1]
    tm, tn = _tile(m, tm_pref), _tile(n, tn_pref)
    osz = jnp.dtype(out_dtype).itemsize
    vmem = 2 * (tm * k * 2 + k * tn * 2 + tm * tn * osz) + tm * tn * 4 + (4 << 20)
    return pl.pallas_call(
        _mm_kernel,
        out_shape=jax.ShapeDtypeStruct((m, n), out_dtype),
        grid=(m // tm, n // tn),
        in_specs=[pl.BlockSpec((tm, k), lambda i, j: (i, 0)),
                  pl.BlockSpec((k, tn), lambda i, j: (0, j))],
        out_specs=pl.BlockSpec((tm, tn), lambda i, j: (i, j)),
        compiler_params=_params(("parallel", "parallel"), vmem),
        name=name,
    )(a, b)


def _da_kernel(parts_ref, lam_ref, q_ref, k_ref, v_ref, w_ref, o_ref, s_ref, p_ref, kaug_ref, l_ref, *,
               tq, hb, nq, lam_init):
    hg = pl.program_id(1)
    qi = pl.program_id(2)
    lp = lam_ref[...]
    lam = (jnp.exp(jnp.sum(lp[0:1] * lp[1:2], axis=-1, keepdims=True))
           - jnp.exp(jnp.sum(lp[2:3] * lp[3:4], axis=-1, keepdims=True)) + lam_init)
    hw = q_ref.shape[1] // hb
    dh = hw // 2
    dv = v_ref.shape[1] // hb
    seq = k_ref.shape[0]
    lane = lax.broadcasted_iota(jnp.int32, (tq, hw), 1)

    def aug_columns(pos, lane_idx, h, key_side):
        hi = (pos >> 8).astype(F32)
        lo = (pos & 255).astype(F32)
        cols = jnp.zeros(pos.shape, F32)
        for i in range(3):
            part = parts_ref[3 * h + i]
            const = (-256.0 * part, -part) if key_side else (256.0 * part, part)
            var = (6 + i, 9 + i) if key_side else (i, 3 + i)
            fixed = (i, 3 + i) if key_side else (6 + i, 9 + i)
            cols = jnp.where(lane_idx == fixed[0], const[0], cols)
            cols = jnp.where(lane_idx == fixed[1], const[1], cols)
            cols = jnp.where(lane_idx == var[0], hi, cols)
            cols = jnp.where(lane_idx == var[1], lo, cols)
        return cols.astype(BF16)

    @pl.when(qi == 0)
    def _():
        kpos = lax.broadcasted_iota(jnp.int32, (seq, LANES), 0)
        klane = lax.broadcasted_iota(jnp.int32, (seq, LANES), 1)
        for hh in range(hb):
            kaug_ref[hh] = aug_columns(kpos, klane, hg * hb + hh, True)

    def run(nblk):
        nk = nblk * tq
        qpos = lax.broadcasted_iota(jnp.int32, (tq, LANES), 0) + (nk - tq)
        qlane = lax.broadcasted_iota(jnp.int32, (tq, LANES), 1)
        row = lax.broadcasted_iota(jnp.int32, (tq, tq), 0)
        col = lax.broadcasted_iota(jnp.int32, (tq, tq), 1)
        future = jnp.concatenate([col > row, col > row], axis=0)
        for hh in range(hb):
            q = q_ref[:, hh * hw:(hh + 1) * hw]
            qs = (q.astype(F32) * (dh ** -0.5)).astype(BF16)
            zero = jnp.zeros_like(qs)
            qaug = aug_columns(qpos, qlane, hg * hb + hh, False)
            qc = jnp.concatenate(
                [jnp.concatenate([jnp.where(lane < dh, qs, zero), qaug], axis=1),
                 jnp.concatenate([jnp.where(lane >= dh, qs, zero), qaug], axis=1)], axis=0)
            k = jnp.concatenate([k_ref[0:nk, hh * hw:(hh + 1) * hw], kaug_ref[hh, 0:nk, :]], axis=1)
            s = lax.dot_general(qc, k, _NT, preferred_element_type=F32)
            last = jnp.where(future, -jnp.inf, s[:, nk - tq:])
            s = last if nblk == 1 else jnp.concatenate([s[:, :nk - tq], last], axis=1)
            s_ref[hh, :, 0:nk] = s

            for r0 in range(0, 2 * tq, SOFTMAX_ROWS):
                sc = s_ref[hh, r0:r0 + SOFTMAX_ROWS, 0:nk]
                p = jnp.exp(sc - jnp.max(sc, axis=-1, keepdims=True))
                l_ref[hh, r0:r0 + SOFTMAX_ROWS, :] = jnp.sum(p, axis=-1, keepdims=True)
                p_ref[hh, r0:r0 + SOFTMAX_ROWS, 0:nk] = p.astype(BF16)
            l = l_ref[hh]
            pv = jnp.dot(p_ref[hh, :, 0:nk], v_ref[0:nk, hh * dv:(hh + 1) * dv],
                         preferred_element_type=F32)
            o = pv[:tq] / l[:tq] - lam * (pv[tq:] / l[tq:])
            ms = jnp.mean(o * o, axis=-1, keepdims=True)
            o = o * (lax.rsqrt(ms + NORM_EPS) * (1.0 - lam_init)) * w_ref[...]
            o_ref[:, hh * dv:(hh + 1) * dv] = o.astype(o_ref.dtype)

    lax.switch(qi, [functools.partial(run, n) for n in range(1, nq + 1)])


def _diff_attention(proj, da_lambda, subln_w, bsz, seq, q_off, k_off, v_off, layer_idx):
    t = bsz * seq
    hw = 2 * DA_DH
    hb = min(2, DA_HEADS)
    ng = DA_HEADS // hb
    gw = hb * hw
    assert hw == DA_DV and q_off % gw == 0 and k_off % gw == 0 and v_off % gw == 0
    tq = _tile(seq, 256)
    nq = seq // tq
    lam_init = 0.8 - 0.6 * math.exp(-0.3 * layer_idx)
    slopes = 2.0 ** (-8.0 * jnp.arange(1, DA_HEADS + 1, dtype=F32) / DA_HEADS)
    p1 = slopes.astype(BF16).astype(F32)
    p2 = (slopes - p1).astype(BF16).astype(F32)
    p3 = (slopes - p1 - p2).astype(BF16).astype(F32)
    parts = jnp.stack([p1, p2, p3], axis=1).reshape(-1)
    qb, kb, vb = q_off // gw, k_off // gw, v_off // gw
    vmem = (2 * (2 * tq * gw * 2 + 2 * seq * gw * 2) + hb * 2 * tq * seq * 6 + hb * seq * LANES * 2
            + 3 * 2 * tq * seq * 4 + (8 << 20))
    return pl.pallas_call(
        functools.partial(_da_kernel, tq=tq, hb=hb, nq=nq, lam_init=lam_init),
        out_shape=jax.ShapeDtypeStruct((t, DA_HEADS * DA_DV), BF16),
        grid=(bsz, ng, nq),
        in_specs=[
            pl.BlockSpec(memory_space=pltpu.SMEM),
            pl.BlockSpec((4, DA_DH), lambda b, g, i: (0, 0)),
            pl.BlockSpec((tq, gw), lambda b, g, i: (b * nq + i, qb + g)),
            pl.BlockSpec((seq, gw), lambda b, g, i: (b, kb + g)),
            pl.BlockSpec((seq, gw), lambda b, g, i: (b, vb + g)),
            pl.BlockSpec((1, DA_DV), lambda b, g, i: (0, 0)),
        ],
        out_specs=pl.BlockSpec((tq, gw), lambda b, g, i: (b * nq + i, g)),
        scratch_shapes=[pltpu.VMEM((hb, 2 * tq, seq), F32), pltpu.VMEM((hb, 2 * tq, seq), BF16),
                        pltpu.VMEM((hb, seq, LANES), BF16), pltpu.VMEM((hb, 2 * tq, 1), F32)],
        compiler_params=_params(("parallel", "parallel", "arbitrary"), vmem),
        name="diffattn",
    )(parts, da_lambda.astype(F32), proj, proj, proj, subln_w.astype(F32).reshape(1, DA_DV))


def _ret_kernel(cd_ref, q_ref, k_ref, v_ref, g_ref, dm_ref, cross_ref, sd_ref, w_ref, b_ref,
                o_ref, state_ref, *, hg):
    grp = pl.program_id(1)
    c = pl.program_id(2)

    @pl.when(c == 0)
    def _():
        state_ref[...] = jnp.zeros_like(state_ref)

    for hh in range(hg):
        qh = q_ref[:, hh * RET_DK:(hh + 1) * RET_DK]
        kh = k_ref[:, hh * RET_DK:(hh + 1) * RET_DK]
        vh = v_ref[:, hh * RET_DV:(hh + 1) * RET_DV]
        st = state_ref[hh]
        inner = lax.dot_general(qh, kh, _NT, preferred_element_type=F32) * dm_ref[hh]
        out = (jnp.dot(inner.astype(BF16), vh, preferred_element_type=F32)
               + jnp.dot(qh, st.astype(BF16), preferred_element_type=F32) * cross_ref[hh])
        ks = (kh.astype(F32) * sd_ref[hh]).astype(BF16)
        state_ref[hh] = (st * cd_ref[grp * hg + hh]
                         + lax.dot_general(ks, vh, _TN, preferred_element_type=F32))
        mu = jnp.mean(out, axis=-1, keepdims=True)
        d = out - mu
        var = jnp.mean(d * d, axis=-1, keepdims=True)
        sl = slice(hh * RET_DV, (hh + 1) * RET_DV)
        y = d * lax.rsqrt(var + NORM_EPS) * w_ref[:, sl] + b_ref[:, sl]
        g = g_ref[:, sl].astype(F32)
        o_ref[:, sl] = (y * (g * _sigmoid(g))).astype(o_ref.dtype)


def _retention(proj, gn_w, gn_b, bsz, seq, q_off, k_off, v_off, g_off):
    t = bsz * seq
    nc = seq // RET_CHUNK
    hg = min(8, RET_HEADS)
    ngrp = RET_HEADS // hg
    qw, vw = hg * RET_DK, hg * RET_DV
    assert q_off % qw == 0 and k_off % qw == 0 and v_off % vw == 0 and g_off % vw == 0
    log_g = jnp.log(1.0 - 2.0 ** (-5.0 - jnp.arange(RET_HEADS, dtype=F32)))
    idx = jnp.arange(RET_CHUNK, dtype=F32)
    rel = idx[:, None] - idx[None, :]
    scale = RET_DK ** -0.5
    dmask = jnp.where(rel >= 0, jnp.exp(log_g[:, None, None] * jnp.maximum(rel, 0.0)), 0.0) * scale
    cross = jnp.exp(log_g[:, None] * (idx + 1.0))[:, :, None]
    sdec = (jnp.exp(log_g[:, None] * (RET_CHUNK - 1.0 - idx)) * scale)[:, :, None]
    cdec = jnp.exp(log_g * RET_CHUNK)
    qb, kb, vb, gb = q_off // qw, k_off // qw, v_off // vw, g_off // vw
    c_ = RET_CHUNK
    vmem = (2 * (2 * c_ * qw * 2 + 3 * c_ * vw * 2 + hg * c_ * c_ * 4 + 2 * hg * c_ * LANES * 4)
            + hg * RET_DK * RET_DV * 4 + (16 << 20))
    return pl.pallas_call(
        functools.partial(_ret_kernel, hg=hg),
        out_shape=jax.ShapeDtypeStruct((t, RET_HEADS * RET_DV), BF16),
        grid=(bsz, ngrp, nc),
        in_specs=[
            pl.BlockSpec(memory_space=pltpu.SMEM),
            pl.BlockSpec((c_, qw), lambda b, g, c: (b * nc + c, qb + g)),
            pl.BlockSpec((c_, qw), lambda b, g, c: (b * nc + c, kb + g)),
            pl.BlockSpec((c_, vw), lambda b, g, c: (b * nc + c, vb + g)),
            pl.BlockSpec((c_, vw), lambda b, g, c: (b * nc + c, gb + g)),
            pl.BlockSpec((hg, c_, c_), lambda b, g, c: (g, 0, 0)),
            pl.BlockSpec((hg, c_, 1), lambda b, g, c: (g, 0, 0)),
            pl.BlockSpec((hg, c_, 1), lambda b, g, c: (g, 0, 0)),
            pl.BlockSpec((1, vw), lambda b, g, c: (0, g)),
            pl.BlockSpec((1, vw), lambda b, g, c: (0, g)),
        ],
        out_specs=pl.BlockSpec((c_, vw), lambda b, g, c: (b * nc + c, g)),
        scratch_shapes=[pltpu.VMEM((hg, RET_DK, RET_DV), F32)],
        compiler_params=_params(("parallel", "parallel", "arbitrary"), vmem),
        name="retention",
    )(cdec, proj, proj, proj, proj, dmask, cross, sdec,
      gn_w.astype(F32).reshape(1, -1), gn_b.astype(F32).reshape(1, -1))


def _merge_kernel(a_ref, wd_ref, r_ref, wr_ref, ga_ref, gr_ref, o_ref):
    a2 = jnp.dot(a_ref[...], wd_ref[...], preferred_element_type=F32)
    r2 = jnp.dot(r_ref[...], wr_ref[...], preferred_element_type=F32)
    ga = ga_ref[...].astype(F32)
    gr = gr_ref[...].astype(F32)
    o_ref[...] = (_sigmoid(ga) * a2 + _sigmoid(gr) * r2).astype(o_ref.dtype)


def _merge(a, wd, r, wr, proj, ga_off, gr_off, d_model):
    t = a.shape[0]
    ka, kr = a.shape[1], r.shape[1]
    tm, tn = _tile(t, 512), _tile(d_model, 1024)
    assert ga_off % tn == 0 and gr_off % tn == 0
    gab, grb = ga_off // tn, gr_off // tn
    vmem = 2 * ((tm + tn) * (ka + kr) * 2 + 3 * tm * tn * 2) + 3 * tm * tn * 4 + (4 << 20)
    return pl.pallas_call(
        _merge_kernel,
        out_shape=jax.ShapeDtypeStruct((t, d_model), BF16),
        grid=(t // tm, d_model // tn),
        in_specs=[
            pl.BlockSpec((tm, ka), lambda i, j: (i, 0)),
            pl.BlockSpec((ka, tn), lambda i, j: (0, j)),
            pl.BlockSpec((tm, kr), lambda i, j: (i, 0)),
            pl.BlockSpec((kr, tn), lambda i, j: (0, j)),
            pl.BlockSpec((tm, tn), lambda i, j: (i, gab + j)),
            pl.BlockSpec((tm, tn), lambda i, j: (i, grb + j)),
        ],
        out_specs=pl.BlockSpec((tm, tn), lambda i, j: (i, j)),
        compiler_params=_params(("parallel", "parallel"), vmem),
        name="merge",
    )(a, wd, r, wr, proj, proj)


def _outln_kernel(m_ref, w_ref, x_ref, lw_ref, lb_ref, of_ref, ob_ref, *, nj, tn):
    j = pl.program_id(1)
    z = DEEPNORM_ALPHA * x_ref[...] + jnp.dot(m_ref[...], w_ref[...], preferred_element_type=F32)

    def store(jj):
        def f():
            of_ref[:, jj * tn:(jj + 1) * tn] = z
        return f

    lax.switch(j, [store(jj) for jj in range(nj)])

    @pl.when(j == nj - 1)
    def _():
        d = nj * tn
        cols = [slice(jj * tn, (jj + 1) * tn) for jj in range(nj)]
        tot = of_ref[:, cols[0]].sum(axis=-1, keepdims=True)
        for sl in cols[1:]:
            tot = tot + of_ref[:, sl].sum(axis=-1, keepdims=True)
        mu = tot / d
        sq = jnp.square(of_ref[:, cols[0]] - mu).sum(axis=-1, keepdims=True)
        for sl in cols[1:]:
            sq = sq + jnp.square(of_ref[:, sl] - mu).sum(axis=-1, keepdims=True)
        rstd = lax.rsqrt(sq / d + NORM_EPS)
        for sl in cols:
            y = (of_ref[:, sl] - mu) * rstd * lw_ref[:, sl] + lb_ref[:, sl]
            of_ref[:, sl] = y
            ob_ref[:, sl] = y.astype(ob_ref.dtype)


def _out_ln(merged, w_out, x2d, ln_w, ln_b):
    t, d = x2d.shape
    tm, tn = _tile(t, 512), _tile(d, 512)
    nj = d // tn
    vmem = (2 * (tm * d * 2 + d * tn * 2 + tm * tn * 4 + tm * d * 4 + tm * d * 2)
            + 3 * tm * tn * 4 + (4 << 20))
    return pl.pallas_call(
        functools.partial(_outln_kernel, nj=nj, tn=tn),
        out_shape=(jax.ShapeDtypeStruct((t, d), F32), jax.ShapeDtypeStruct((t, d), BF16)),
        grid=(t // tm, nj),
        in_specs=[
            pl.BlockSpec((tm, d), lambda i, j: (i, 0)),
            pl.BlockSpec((d, tn), lambda i, j: (0, j)),
            pl.BlockSpec((tm, tn), lambda i, j: (i, j)),
            pl.BlockSpec((1, d), lambda i, j: (0, 0)),
            pl.BlockSpec((1, d), lambda i, j: (0, 0)),
        ],
        out_specs=(pl.BlockSpec((tm, d), lambda i, j: (i, 0)),
                   pl.BlockSpec((tm, d), lambda i, j: (i, 0))),
        compiler_params=_params(("parallel", "arbitrary"), vmem),
        name="outln",
    )(merged, w_out, x2d, ln_w.astype(F32).reshape(1, d), ln_b.astype(F32).reshape(1, d))


_NOT_TOP = 99.0


def _top16(s, exact):
    nk, tl = s.shape
    key = lax.broadcasted_iota(jnp.int32, (nk, tl), 0).astype(F32)
    rank = jnp.full((nk, tl), _NOT_TOP, F32)
    vals = []
    for r in range(PEER_TOPK):
        m = jnp.max(s, axis=0, keepdims=True)
        if exact:
            first = jnp.min(jnp.where(s == m, key, float(nk)), axis=0, keepdims=True)
            hit = key == first
        else:
            hit = s == m
        rank = jnp.where(hit, float(r), rank)
        s = jnp.where(hit, -jnp.inf, s)
        vals.append(m)
    return jnp.concatenate(vals, axis=0), rank


def _route_head(s1, s2, exact):
    tl = s1.shape[1]
    vals, rank = _top16(jnp.concatenate([s1, s2], axis=1), exact)
    a, rank1 = vals[:, :tl], rank[:, :tl]
    b, rank2 = vals[:, tl:], rank[:, tl:]
    k = PEER_TOPK
    ea = jnp.exp(a - a[0:1])
    eb = jnp.exp(b - b[0:1])

    half = k // 2
    i_k = lax.broadcasted_iota(jnp.int32, (k, tl), 0).astype(F32)
    i_h = lax.broadcasted_iota(jnp.int32, (half, tl), 0).astype(F32)
    cand = [a[0:1] + b]
    prod = [ea[0:1] * eb]
    pos = [i_k]
    for r1 in range(1, half):
        cand.append(a[r1:r1 + 1] + b[0:half])
        prod.append(ea[r1:r1 + 1] * eb[0:half])
        pos.append(i_h + float(r1 * k))
    cand.append(a[half:k] + b[0:1])
    prod.append(ea[half:k] * eb[0:1])
    pos.append((i_h + float(half)) * float(k))
    cand = jnp.concatenate(cand, axis=0)
    prod = jnp.concatenate(prod, axis=0)
    pos = jnp.concatenate(pos, axis=0)
    picked = jnp.zeros_like(cand)
    for _ in range(k):
        m = jnp.max(cand, axis=0, keepdims=True)
        if exact:
            first = jnp.min(jnp.where(cand == m, pos, float(k * k)), axis=0, keepdims=True)
            hit = pos == first
        else:
            hit = cand == m
        picked = jnp.where(hit, 1.0, picked)
        cand = jnp.where(hit, -jnp.inf, cand)
    z = jnp.sum(picked * prod, axis=0, keepdims=True)

    counts = [jnp.sum(picked[0:k], axis=0, keepdims=True)]
    for r1 in range(1, half):
        lo = k + (r1 - 1) * half
        counts.append(jnp.sum(picked[lo:lo + half], axis=0, keepdims=True))
    lo = k + (half - 1) * half
    for r in range(half):
        counts.append(picked[lo + r:lo + r + 1])
    n = jnp.zeros_like(s1)
    for r1 in range(k):
        n = jnp.where(rank1 == float(r1), counts[r1], n)

    ranked = jnp.sum(jnp.where(rank != _NOT_TOP, 1.0, 0.0), axis=0, keepdims=True)
    chosen = ranked[:, :tl] + ranked[:, tl:] + jnp.sum(picked, axis=0, keepdims=True)
    return rank2, jnp.exp(s2 - b[0:1]), n, jnp.exp(s1 - a[0:1]) / z, chosen


def _route_kernel(x_ref, wq_ref, keys_ref, r2_ref, e2_ref, n_ref, c_ref, q_scr, *, hs):
    step = pl.program_id(1)
    ngrp = q_scr.shape[0]
    dk = q_scr.shape[2]

    @pl.when(step == 0)
    def _():
        q = jnp.dot(x_ref[...], wq_ref[...], preferred_element_type=F32)
        for g in range(ngrp):
            q_scr[g] = q[:, g * dk:(g + 1) * dk].astype(q_scr.dtype)

    tl = x_ref.shape[0]
    s1 = jnp.concatenate([lax.dot_general(keys_ref[2 * (step * hs + hh)], q_scr[2 * (step * hs + hh)], _NT,
                                          preferred_element_type=F32) for hh in range(hs)], axis=1)
    s2 = jnp.concatenate([lax.dot_general(keys_ref[2 * (step * hs + hh) + 1], q_scr[2 * (step * hs + hh) + 1],
                                          _NT, preferred_element_type=F32) for hh in range(hs)], axis=1)

    def write(rank2, e2, n, c):
        for hh in range(hs):
            sl = slice(hh * tl, (hh + 1) * tl)
            r2_ref[hh] = rank2[:, sl].astype(r2_ref.dtype)
            e2_ref[hh] = e2[:, sl].astype(e2_ref.dtype)
            n_ref[hh] = n[:, sl]
            c_ref[hh] = c[:, sl]

    fast = _route_head(s1, s2, exact=False)
    no_ties = jnp.max(fast[4]) == 3.0 * PEER_TOPK
    lax.cond(no_ties, lambda: write(*fast[:4]), lambda: write(*_route_head(s1, s2, exact=True)[:4]))


def _route(x1b, wq, keys):
    t, d = x1b.shape
    nq = wq.shape[1]
    ngrp = 2 * PEER_HEADS
    assert nq == ngrp * PEER_DHALF and PEER_TOPK == 16 and PEER_NKEYS % 8 == 0
    tl = _tile(t, 256)
    hs = min(2, PEER_HEADS)
    shp = jax.ShapeDtypeStruct((PEER_HEADS, PEER_NKEYS, t), F32)
    shp_b = jax.ShapeDtypeStruct((PEER_HEADS, PEER_NKEYS, t), BF16)
    ospec = pl.BlockSpec((hs, PEER_NKEYS, tl), lambda i, h: (h, 0, i))
    vmem = (2 * tl * d * 2 + d * nq * 2 + 2 * ngrp * PEER_NKEYS * PEER_DHALF * 2
            + 8 * PEER_NKEYS * tl * 4 + ngrp * tl * PEER_DHALF * 2 + tl * nq * 4 + (16 << 20))
    return pl.pallas_call(
        functools.partial(_route_kernel, hs=hs),
        out_shape=(shp_b, shp_b, shp, shp),
        grid=(t // tl, PEER_HEADS // hs),
        in_specs=[
            pl.BlockSpec((tl, d), lambda i, h: (i, 0)),
            pl.BlockSpec((d, nq), lambda i, h: (0, 0), pipeline_mode=pl.Buffered(1)),
            pl.BlockSpec((ngrp, PEER_NKEYS, PEER_DHALF), lambda i, h: (0, 0, 0)),
        ],
        out_specs=(ospec, ospec, ospec, ospec),
        scratch_shapes=[pltpu.VMEM((ngrp, tl, PEER_DHALF), BF16)],
        compiler_params=_params(("parallel", "arbitrary"), vmem),
        name="route",
    )(x1b, wq, keys)


def _vt_kernel(v_ref, o_ref):
    o_ref[...] = v_ref[...].T.astype(o_ref.dtype)


def _transpose_to_bf16(v):
    ne, d = v.shape
    te = _tile(ne, 256)
    vmem = 2 * (te * d * 4 + d * te * 2) + 3 * te * d * 4 + (2 << 20)
    return pl.pallas_call(
        _vt_kernel,
        out_shape=jax.ShapeDtypeStruct((d, ne), BF16),
        grid=(ne // te,),
        in_specs=[pl.BlockSpec((te, d), lambda i: (i, 0))],
        out_specs=pl.BlockSpec((d, te), lambda i: (0, i)),
        compiler_params=_params(("parallel",), vmem),
        name="vtranspose",
    )(v)


def _peer_kernel(x_ref, u_ref, vt_ref, r2_ref, e2_ref, n_ref, c_ref, yt_ref, g_ref, *, rows):
    j = pl.program_id(1)
    nk = PEER_NKEYS

    @pl.when(j == 0)
    def _():
        yt_ref[...] = jnp.zeros_like(yt_ref)

    for a in range(rows):
        i1 = j * rows + a
        g = None
        for h in range(PEER_HEADS):
            nrow = n_ref[h, pl.ds(i1, 1), :].astype(BF16)
            crow = c_ref[h, pl.ds(i1, 1), :].astype(BF16)
            term = jnp.where(r2_ref[h] < nrow, e2_ref[h] * crow, jnp.zeros((), BF16))
            g = term if g is None else g + term
        g_ref[a * nk:(a + 1) * nk, :] = g

    te = rows * nk
    hte = te // 2
    x = x_ref[...]
    acts = []
    for s in range(2):
        sl = slice(s * hte, (s + 1) * hte)
        ht = lax.dot_general(u_ref[sl, :], x, _NT, preferred_element_type=F32)
        gelu = ht * (lax.erf(ht * (2.0 ** -0.5)) + 1.0) * 0.5
        acts.append(gelu.astype(BF16) * g_ref[sl, :])
    yt_ref[...] += jnp.dot(vt_ref[...], jnp.concatenate(acts, axis=0), preferred_element_type=F32)


def _peer_dense(x1b, u, vt, r2, e2, nn, cc):
    t, d = x1b.shape
    ne = u.shape[0]
    assert ne == PEER_NKEYS * PEER_NKEYS and vt.shape == (d, ne)
    tm = _tile(t, 512)
    rows = 4
    te = rows * PEER_NKEYS
    hk = PEER_HEADS * PEER_NKEYS
    bspec = pl.BlockSpec((PEER_HEADS, PEER_NKEYS, tm), lambda i, j: (0, 0, i),
                         pipeline_mode=pl.Buffered(1))
    vmem = (tm * d * 2 + 4 * te * d * 2 + 2 * hk * tm * 2 + 2 * hk * tm * 4 + 2 * d * tm * 4
            + te * tm * 2 + 3 * te * tm * 4 + (4 << 20))
    return pl.pallas_call(
        functools.partial(_peer_kernel, rows=rows),
        out_shape=jax.ShapeDtypeStruct((d, t), F32),
        grid=(t // tm, ne // te),
        in_specs=[
            pl.BlockSpec((tm, d), lambda i, j: (i, 0), pipeline_mode=pl.Buffered(1)),
            pl.BlockSpec((te, d), lambda i, j: (j, 0)),
            pl.BlockSpec((d, te), lambda i, j: (0, j)),
            bspec, bspec, bspec, bspec,
        ],
        out_specs=pl.BlockSpec((d, tm), lambda i, j: (0, i)),
        scratch_shapes=[pltpu.VMEM((te, tm), BF16)],
        compiler_params=_params(("parallel", "arbitrary"), vmem),
        name="peer",
    )(x1b, u, vt, r2, e2, nn, cc)


def _ln2_kernel(x_ref, yt_ref, w_ref, b_ref, o_ref):
    z = DEEPNORM_ALPHA * x_ref[...] + yt_ref[...].T
    mu = jnp.mean(z, axis=-1, keepdims=True)
    d = z - mu
    var = jnp.mean(d * d, axis=-1, keepdims=True)
    o_ref[...] = (d * lax.rsqrt(var + NORM_EPS) * w_ref[...] + b_ref[...]).astype(o_ref.dtype)


def _ln2(x1f, yt, ln_w, ln_b):
    t, d = x1f.shape
    tm = _tile(t, 256)
    vmem = 2 * 3 * tm * d * 4 + 3 * tm * d * 4 + (4 << 20)
    return pl.pallas_call(
        _ln2_kernel,
        out_shape=jax.ShapeDtypeStruct((t, d), F32),
        grid=(t // tm,),
        in_specs=[
            pl.BlockSpec((tm, d), lambda i: (i, 0)),
            pl.BlockSpec((d, tm), lambda i: (0, i)),
            pl.BlockSpec((1, d), lambda i: (0, 0)),
            pl.BlockSpec((1, d), lambda i: (0, 0)),
        ],
        out_specs=pl.BlockSpec((tm, d), lambda i: (i, 0)),
        compiler_params=_params(("parallel",), vmem),
        name="ln2",
    )(x1f, yt, ln_w.astype(F32).reshape(1, d), ln_b.astype(F32).reshape(1, d))


def kernel(x, w_in, da_lambda, da_subln_w, w_da_out, ret_gn_w, ret_gn_b, w_ret_out, w_out,
           ln1_w, ln1_b, peer_w_query, peer_sub_keys, peer_u, peer_v, ln2_w, ln2_b):
    bsz, seq, d = x.shape
    t = bsz * seq
    da_qk_w, da_v_w = DA_HEADS * 2 * DA_DH, DA_HEADS * DA_DV
    ret_qk_w, ret_v_w = RET_HEADS * RET_DK, RET_HEADS * RET_DV
    splits = (da_qk_w, da_qk_w, da_v_w, ret_qk_w, ret_qk_w, ret_v_w, ret_v_w, d, d)
    offs = [0]
    for s in splits:
        offs.append(offs[-1] + s)
    o_q, o_k, o_v, o_rq, o_rk, o_rv, o_rg, o_ga, o_gr = offs[:-1]

    xf = x.reshape(t, d)
    for l in range(DEPTH):
        xb = xf.astype(BF16)
        proj = _matmul(xb, w_in[l].astype(BF16), BF16, "proj")
        a = _diff_attention(proj, da_lambda[l], da_subln_w[l], bsz, seq, o_q, o_k, o_v, l)
        r = _retention(proj, ret_gn_w[l], ret_gn_b[l], bsz, seq, o_rq, o_rk, o_rv, o_rg)
        merged = _merge(a, w_da_out[l].astype(BF16), r, w_ret_out[l].astype(BF16), proj, o_ga, o_gr, d)
        x1f, x1b = _out_ln(merged, w_out[l].astype(BF16), xf, ln1_w[l], ln1_b[l])
        keys = peer_sub_keys[l].astype(BF16).reshape(2 * PEER_HEADS, PEER_NKEYS, PEER_DHALF)
        r2, e2, nn, cc = _route(x1b, peer_w_query[l].astype(BF16), keys)
        yt = _peer_dense(x1b, peer_u[l].astype(BF16), _transpose_to_bf16(peer_v[l]), r2, e2, nn, cc)
        xf = _ln2(x1f, yt, ln2_w[l], ln2_b[l])
    return xf.reshape(bsz, seq, d)
```

```python
import functools
import math

import jax
import jax.numpy as jnp
from jax import lax
from jax.experimental import pallas as pl
from jax.experimental.pallas import tpu as pltpu

DEPTH = 1
DA_HEADS = 16
DA_DH = 64
DA_DV = 2 * DA_DH
RET_HEADS = 16
RET_DK = 128
RET_DV = 256
RET_CHUNK = 128
PEER_HEADS = 8
PEER_NKEYS = 128
PEER_DHALF = 128
PEER_TOPK = 16
NORM_EPS = 1e-5
DEEPNORM_ALPHA = (2.0 * DEPTH) ** 0.25

V7X_VMEM_BYTES = 64 * 1024 * 1024
LANES = 128
SOFTMAX_ROWS = 128
PEER_SPLIT = 2
F32 = jnp.float32
BF16 = jnp.bfloat16

_NT = (((1,), (1,)), ((), ()))
_TN = (((0,), (0,)), ((), ()))


def _tile(n, pref):
    if n <= pref:
        return n
    t = pref - pref % LANES
    while t >= LANES:
        if n % t == 0:
            return t
        t -= LANES
    raise ValueError(f"no tile for {n} <= {pref}")


def _params(sem, vmem_bytes, flags=None):
    limit = min(int(vmem_bytes), V7X_VMEM_BYTES - 4 * 1024 * 1024)
    return pltpu.CompilerParams(dimension_semantics=sem, vmem_limit_bytes=limit, flags=flags)


def _sigmoid(x):
    return 0.5 * jnp.tanh(0.5 * x) + 0.5


def _mm_kernel(a_ref, b_ref, o_ref):
    o_ref[...] = jnp.dot(a_ref[...], b_ref[...], preferred_element_type=F32).astype(o_ref.dtype)


def _matmul(a, b, out_dtype, name, tm_pref=1024, tn_pref=1024):
    m, k = a.shape
    n = b.shape[1]
    tm, tn = _tile(m, tm_pref), _tile(n, tn_pref)
    osz = jnp.dtype(out_dtype).itemsize
    vmem = 2 * (tm * k * 2 + k * tn * 2 + tm * tn * osz) + tm * tn * 4 + (4 << 20)
    return pl.pallas_call(
        _mm_kernel,
        out_shape=jax.ShapeDtypeStruct((m, n), out_dtype),
        grid=(m // tm, n // tn),
        in_specs=[pl.BlockSpec((tm, k), lambda i, j: (i, 0)),
                  pl.BlockSpec((k, tn), lambda i, j: (0, j))],
        out_specs=pl.BlockSpec((tm, tn), lambda i, j: (i, j)),
        compiler_params=_params(("parallel", "parallel"), vmem),
        name=name,
    )(a, b)


def _da_kernel(parts_ref, lam_ref, q_ref, k_ref, v_ref, w_ref, o_ref, s_ref, p_ref, kaug_ref, vext_ref, *,
               tq, hb, nq, lam_init):
    hg = pl.program_id(1)
    qi = pl.program_id(2)
    lp = lam_ref[...]
    lam = (jnp.exp(jnp.sum(lp[0:1] * lp[1:2], axis=-1, keepdims=True))
           - jnp.exp(jnp.sum(lp[2:3] * lp[3:4], axis=-1, keepdims=True)) + lam_init)
    hw = q_ref.shape[1] // hb
    dh = hw // 2
    dv = v_ref.shape[1] // hb
    seq = k_ref.shape[0]
    lane = lax.broadcasted_iota(jnp.int32, (tq, hw), 1)

    def aug_columns(pos, lane_idx, h, key_side):
        hi = (pos >> 8).astype(F32)
        lo = (pos & 255).astype(F32)
        cols = jnp.zeros(pos.shape, F32)
        for i in range(3):
            part = parts_ref[3 * h + i]
            const = (-256.0 * part, -part) if key_side else (256.0 * part, part)
            var = (6 + i, 9 + i) if key_side else (i, 3 + i)
            fixed = (i, 3 + i) if key_side else (6 + i, 9 + i)
            cols = jnp.where(lane_idx == fixed[0], const[0], cols)
            cols = jnp.where(lane_idx == fixed[1], const[1], cols)
            cols = jnp.where(lane_idx == var[0], hi, cols)
            cols = jnp.where(lane_idx == var[1], lo, cols)
        return cols.astype(BF16)

    @pl.when(qi == 0)
    def _():
        kpos = lax.broadcasted_iota(jnp.int32, (seq, LANES), 0)
        klane = lax.broadcasted_iota(jnp.int32, (seq, LANES), 1)
        ones_col = jnp.where(klane == 0, 1.0, 0.0).astype(BF16)
        for hh in range(hb):
            kaug_ref[hh] = aug_columns(kpos, klane, hg * hb + hh, True)
            vext_ref[hh] = jnp.concatenate([v_ref[:, hh * dv:(hh + 1) * dv], ones_col], axis=1)

    def run(nblk):
        nk = nblk * tq
        qpos = lax.broadcasted_iota(jnp.int32, (tq, LANES), 0) + (nk - tq)
        qlane = lax.broadcasted_iota(jnp.int32, (tq, LANES), 1)
        row = lax.broadcasted_iota(jnp.int32, (tq, tq), 0)
        col = lax.broadcasted_iota(jnp.int32, (tq, tq), 1)
        future = jnp.concatenate([col > row, col > row], axis=0)
        for hh in range(hb):
            q = q_ref[:, hh * hw:(hh + 1) * hw]
            qs = (q.astype(F32) * (dh ** -0.5)).astype(BF16)
            zero = jnp.zeros_like(qs)
            qaug = aug_columns(qpos, qlane, hg * hb + hh, False)
            qc = jnp.concatenate(
                [jnp.concatenate([jnp.where(lane < dh, qs, zero), qaug], axis=1),
                 jnp.concatenate([jnp.where(lane >= dh, qs, zero), qaug], axis=1)], axis=0)
            k = jnp.concatenate([k_ref[0:nk, hh * hw:(hh + 1) * hw], kaug_ref[hh, 0:nk, :]], axis=1)
            s = lax.dot_general(qc, k, _NT, preferred_element_type=F32)
            last = jnp.where(future, -jnp.inf, s[:, nk - tq:])
            s = last if nblk == 1 else jnp.concatenate([s[:, :nk - tq], last], axis=1)
            s_ref[hh, :, 0:nk] = s

            for r0 in range(0, 2 * tq, SOFTMAX_ROWS):
                sc = s_ref[hh, r0:r0 + SOFTMAX_ROWS, 0:nk]
                p = jnp.exp(sc - jnp.max(sc, axis=-1, keepdims=True))
                p_ref[hh, r0:r0 + SOFTMAX_ROWS, 0:nk] = p.astype(BF16)
            pv = jnp.dot(p_ref[hh, :, 0:nk], vext_ref[hh, 0:nk, :],
                         preferred_element_type=F32)
            l = pv[:, dv:dv + 1]
            o = pv[:tq, :dv] / l[:tq] - lam * (pv[tq:, :dv] / l[tq:])
            ms = jnp.mean(o * o, axis=-1, keepdims=True)
            o = o * (lax.rsqrt(ms + NORM_EPS) * (1.0 - lam_init)) * w_ref[...]
            o_ref[:, hh * dv:(hh + 1) * dv] = o.astype(o_ref.dtype)

    lax.switch(qi, [functools.partial(run, n) for n in range(1, nq + 1)])


def _diff_attention(proj, da_lambda, subln_w, bsz, seq, q_off, k_off, v_off, layer_idx):
    t = bsz * seq
    hw = 2 * DA_DH
    hb = min(2, DA_HEADS)
    ng = DA_HEADS // hb
    gw = hb * hw
    assert hw == DA_DV and q_off % gw == 0 and k_off % gw == 0 and v_off % gw == 0
    tq = _tile(seq, 256)
    nq = seq // tq
    lam_init = 0.8 - 0.6 * math.exp(-0.3 * layer_idx)
    slopes = 2.0 ** (-8.0 * jnp.arange(1, DA_HEADS + 1, dtype=F32) / DA_HEADS)
    p1 = slopes.astype(BF16).astype(F32)
    p2 = (slopes - p1).astype(BF16).astype(F32)
    p3 = (slopes - p1 - p2).astype(BF16).astype(F32)
    parts = jnp.stack([p1, p2, p3], axis=1).reshape(-1)
    qb, kb, vb = q_off // gw, k_off // gw, v_off // gw
    vmem = (2 * (2 * tq * gw * 2 + 2 * seq * gw * 2) + hb * 2 * tq * seq * 6 + hb * seq * LANES * 2
            + 3 * 2 * tq * seq * 4 + (8 << 20))
    return pl.pallas_call(
        functools.partial(_da_kernel, tq=tq, hb=hb, nq=nq, lam_init=lam_init),
        out_shape=jax.ShapeDtypeStruct((t, DA_HEADS * DA_DV), BF16),
        grid=(bsz, ng, nq),
        in_specs=[
            pl.BlockSpec(memory_space=pltpu.SMEM),
            pl.BlockSpec((4, DA_DH), lambda b, g, i: (0, 0)),
            pl.BlockSpec((tq, gw), lambda b, g, i: (b * nq + i, qb + g)),
            pl.BlockSpec((seq, gw), lambda b, g, i: (b, kb + g)),
            pl.BlockSpec((seq, gw), lambda b, g, i: (b, vb + g)),
            pl.BlockSpec((1, DA_DV), lambda b, g, i: (0, 0)),
        ],
        out_specs=pl.BlockSpec((tq, gw), lambda b, g, i: (b * nq + i, g)),
        scratch_shapes=[pltpu.VMEM((hb, 2 * tq, seq), F32), pltpu.VMEM((hb, 2 * tq, seq), BF16),
                        pltpu.VMEM((hb, seq, LANES), BF16), pltpu.VMEM((hb, seq, DA_DV + LANES), BF16)],
        compiler_params=_params(("parallel", "parallel", "arbitrary"), vmem),
        name="diffattn",
    )(parts, da_lambda.astype(F32), proj, proj, proj, subln_w.astype(F32).reshape(1, DA_DV))


def _ret_kernel(cd_ref, q_ref, k_ref, v_ref, g_ref, dm_ref, cross_ref, sd_ref, w_ref, b_ref,
                o_ref, state_ref, *, hg):
    grp = pl.program_id(1)
    c = pl.program_id(2)

    @pl.when(c == 0)
    def _():
        state_ref[...] = jnp.zeros_like(state_ref)

    for hh in range(hg):
        qh = q_ref[:, hh * RET_DK:(hh + 1) * RET_DK]
        kh = k_ref[:, hh * RET_DK:(hh + 1) * RET_DK]
        vh = v_ref[:, hh * RET_DV:(hh + 1) * RET_DV]
        st = state_ref[hh]
        inner = lax.dot_general(qh, kh, _NT, preferred_element_type=F32) * dm_ref[hh]
        out = (jnp.dot(inner.astype(BF16), vh, preferred_element_type=F32)
               + jnp.dot(qh, st.astype(BF16), preferred_element_type=F32) * cross_ref[hh])
        ks = (kh.astype(F32) * sd_ref[hh]).astype(BF16)
        state_ref[hh] = (st * cd_ref[grp * hg + hh]
                         + lax.dot_general(ks, vh, _TN, preferred_element_type=F32))
        mu = jnp.mean(out, axis=-1, keepdims=True)
        d = out - mu
        var = jnp.mean(d * d, axis=-1, keepdims=True)
        sl = slice(hh * RET_DV, (hh + 1) * RET_DV)
        y = d * lax.rsqrt(var + NORM_EPS) * w_ref[:, sl] + b_ref[:, sl]
        g = g_ref[:, sl].astype(F32)
        o_ref[:, sl] = (y * (g * _sigmoid(g))).astype(o_ref.dtype)


def _retention(proj, gn_w, gn_b, bsz, seq, q_off, k_off, v_off, g_off):
    t = bsz * seq
    nc = seq // RET_CHUNK
    hg = min(8, RET_HEADS)
    ngrp = RET_HEADS // hg
    qw, vw = hg * RET_DK, hg * RET_DV
    assert q_off % qw == 0 and k_off % qw == 0 and v_off % vw == 0 and g_off % vw == 0
    log_g = jnp.log(1.0 - 2.0 ** (-5.0 - jnp.arange(RET_HEADS, dtype=F32)))
    idx = jnp.arange(RET_CHUNK, dtype=F32)
    rel = idx[:, None] - idx[None, :]
    scale = RET_DK ** -0.5
    dmask = jnp.where(rel >= 0, jnp.exp(log_g[:, None, None] * jnp.maximum(rel, 0.0)), 0.0) * scale
    cross = jnp.exp(log_g[:, None] * (idx + 1.0))[:, :, None]
    sdec = (jnp.exp(log_g[:, None] * (RET_CHUNK - 1.0 - idx)) * scale)[:, :, None]
    cdec = jnp.exp(log_g * RET_CHUNK)
    qb, kb, vb, gb = q_off // qw, k_off // qw, v_off // vw, g_off // vw
    c_ = RET_CHUNK
    vmem = (2 * (2 * c_ * qw * 2 + 3 * c_ * vw * 2 + hg * c_ * c_ * 4 + 2 * hg * c_ * LANES * 4)
            + hg * RET_DK * RET_DV * 4 + (16 << 20))
    return pl.pallas_call(
        functools.partial(_ret_kernel, hg=hg),
        out_shape=jax.ShapeDtypeStruct((t, RET_HEADS * RET_DV), BF16),
        grid=(bsz, ngrp, nc),
        in_specs=[
            pl.BlockSpec(memory_space=pltpu.SMEM),
            pl.BlockSpec((c_, qw), lambda b, g, c: (b * nc + c, qb + g)),
            pl.BlockSpec((c_, qw), lambda b, g, c: (b * nc + c, kb + g)),
            pl.BlockSpec((c_, vw), lambda b, g, c: (b * nc + c, vb + g)),
            pl.BlockSpec((c_, vw), lambda b, g, c: (b * nc + c, gb + g)),
            pl.BlockSpec((hg, c_, c_), lambda b, g, c: (g, 0, 0)),
            pl.BlockSpec((hg, c_, 1), lambda b, g, c: (g, 0, 0)),
            pl.BlockSpec((hg, c_, 1), lambda b, g, c: (g, 0, 0)),
            pl.BlockSpec((1, vw), lambda b, g, c: (0, g)),
            pl.BlockSpec((1, vw), lambda b, g, c: (0, g)),
        ],
        out_specs=pl.BlockSpec((c_, vw), lambda b, g, c: (b * nc + c, g)),
        scratch_shapes=[pltpu.VMEM((hg, RET_DK, RET_DV), F32)],
        compiler_params=_params(("parallel", "parallel", "arbitrary"), vmem),
        name="retention",
    )(cdec, proj, proj, proj, proj, dmask, cross, sdec,
      gn_w.astype(F32).reshape(1, -1), gn_b.astype(F32).reshape(1, -1))


def _merge_kernel(a_ref, wd_ref, r_ref, wr_ref, ga_ref, gr_ref, o_ref):
    a2 = jnp.dot(a_ref[...], wd_ref[...], preferred_element_type=F32)
    r2 = jnp.dot(r_ref[...], wr_ref[...], preferred_element_type=F32)
    ga = ga_ref[...].astype(F32)
    gr = gr_ref[...].astype(F32)
    o_ref[...] = (_sigmoid(ga) * a2 + _sigmoid(gr) * r2).astype(o_ref.dtype)


def _merge(a, wd, r, wr, proj, ga_off, gr_off, d_model):
    t = a.shape[0]
    ka, kr = a.shape[1], r.shape[1]
    tm, tn = _tile(t, 512), _tile(d_model, 1024)
    assert ga_off % tn == 0 and gr_off % tn == 0
    gab, grb = ga_off // tn, gr_off // tn
    vmem = 2 * ((tm + tn) * (ka + kr) * 2 + 3 * tm * tn * 2) + 3 * tm * tn * 4 + (4 << 20)
    return pl.pallas_call(
        _merge_kernel,
        out_shape=jax.ShapeDtypeStruct((t, d_model), BF16),
        grid=(t // tm, d_model // tn),
        in_specs=[
            pl.BlockSpec((tm, ka), lambda i, j: (i, 0)),
            pl.BlockSpec((ka, tn), lambda i, j: (0, j)),
            pl.BlockSpec((tm, kr), lambda i, j: (i, 0)),
            pl.BlockSpec((kr, tn), lambda i, j: (0, j)),
            pl.BlockSpec((tm, tn), lambda i, j: (i, gab + j)),
            pl.BlockSpec((tm, tn), lambda i, j: (i, grb + j)),
        ],
        out_specs=pl.BlockSpec((tm, tn), lambda i, j: (i, j)),
        compiler_params=_params(("parallel", "parallel"), vmem),
        name="merge",
    )(a, wd, r, wr, proj, proj)


def _outln_kernel(m_ref, w_ref, x_ref, lw_ref, lb_ref, of_ref, ob_ref, *, nj, tn):
    j = pl.program_id(1)
    z = DEEPNORM_ALPHA * x_ref[...] + jnp.dot(m_ref[...], w_ref[...], preferred_element_type=F32)

    def store(jj):
        def f():
            of_ref[:, jj * tn:(jj + 1) * tn] = z
        return f

    lax.switch(j, [store(jj) for jj in range(nj)])

    @pl.when(j == nj - 1)
    def _():
        d = nj * tn
        cols = [slice(jj * tn, (jj + 1) * tn) for jj in range(nj)]
        tot = of_ref[:, cols[0]].sum(axis=-1, keepdims=True)
        for sl in cols[1:]:
            tot = tot + of_ref[:, sl].sum(axis=-1, keepdims=True)
        mu = tot / d
        sq = jnp.square(of_ref[:, cols[0]] - mu).sum(axis=-1, keepdims=True)
        for sl in cols[1:]:
            sq = sq + jnp.square(of_ref[:, sl] - mu).sum(axis=-1, keepdims=True)
        rstd = lax.rsqrt(sq / d + NORM_EPS)
        for sl in cols:
            y = (of_ref[:, sl] - mu) * rstd * lw_ref[:, sl] + lb_ref[:, sl]
            of_ref[:, sl] = y
            ob_ref[:, sl] = y.astype(ob_ref.dtype)


def _out_ln(merged, w_out, x2d, ln_w, ln_b):
    t, d = x2d.shape
    tm, tn = _tile(t, 512), _tile(d, 512)
    nj = d // tn
    vmem = (2 * (tm * d * 2 + d * tn * 2 + tm * tn * 4 + tm * d * 4 + tm * d * 2)
            + 3 * tm * tn * 4 + (4 << 20))
    return pl.pallas_call(
        functools.partial(_outln_kernel, nj=nj, tn=tn),
        out_shape=(jax.ShapeDtypeStruct((t, d), F32), jax.ShapeDtypeStruct((t, d), BF16)),
        grid=(t // tm, nj),
        in_specs=[
            pl.BlockSpec((tm, d), lambda i, j: (i, 0)),
            pl.BlockSpec((d, tn), lambda i, j: (0, j)),
            pl.BlockSpec((tm, tn), lambda i, j: (i, j)),
            pl.BlockSpec((1, d), lambda i, j: (0, 0)),
            pl.BlockSpec((1, d), lambda i, j: (0, 0)),
        ],
        out_specs=(pl.BlockSpec((tm, d), lambda i, j: (i, 0)),
                   pl.BlockSpec((tm, d), lambda i, j: (i, 0))),
        compiler_params=_params(("parallel", "arbitrary"), vmem),
        name="outln",
    )(merged, w_out, x2d, ln_w.astype(F32).reshape(1, d), ln_b.astype(F32).reshape(1, d))


_NOT_TOP = 99.0


def _top16(s, exact):
    nk, tl = s.shape
    key = lax.broadcasted_iota(jnp.int32, (nk, tl), 0).astype(F32)
    rank = jnp.full((nk, tl), _NOT_TOP, F32)
    vals = []
    for r in range(PEER_TOPK):
        m = jnp.max(s, axis=0, keepdims=True)
        if exact:
            first = jnp.min(jnp.where(s == m, key, float(nk)), axis=0, keepdims=True)
            hit = key == first
        else:
            hit = s == m
        rank = jnp.where(hit, float(r), rank)
        s = jnp.where(hit, -jnp.inf, s)
        vals.append(m)
    return jnp.concatenate(vals, axis=0), rank


def _route_head(s1, s2, exact):
    tl = s1.shape[1]
    vals, rank = _top16(jnp.concatenate([s1, s2], axis=1), exact)
    a, rank1 = vals[:, :tl], rank[:, :tl]
    b, rank2 = vals[:, tl:], rank[:, tl:]
    k = PEER_TOPK
    ea = jnp.exp(a - a[0:1])
    eb = jnp.exp(b - b[0:1])

    half = k // 2
    i_k = lax.broadcasted_iota(jnp.int32, (k, tl), 0).astype(F32)
    i_h = lax.broadcasted_iota(jnp.int32, (half, tl), 0).astype(F32)
    cand = [a[0:1] + b]
    prod = [ea[0:1] * eb]
    pos = [i_k]
    for r1 in range(1, half):
        cand.append(a[r1:r1 + 1] + b[0:half])
        prod.append(ea[r1:r1 + 1] * eb[0:half])
        pos.append(i_h + float(r1 * k))
    cand.append(a[half:k] + b[0:1])
    prod.append(ea[half:k] * eb[0:1])
    pos.append((i_h + float(half)) * float(k))
    cand = jnp.concatenate(cand, axis=0)
    prod = jnp.concatenate(prod, axis=0)
    pos = jnp.concatenate(pos, axis=0)
    picked = jnp.zeros_like(cand)
    for _ in range(k):
        m = jnp.max(cand, axis=0, keepdims=True)
        if exact:
            first = jnp.min(jnp.where(cand == m, pos, float(k * k)), axis=0, keepdims=True)
            hit = pos == first
        else:
            hit = cand == m
        picked = jnp.where(hit, 1.0, picked)
        cand = jnp.where(hit, -jnp.inf, cand)
    z = jnp.sum(picked * prod, axis=0, keepdims=True)

    counts = [jnp.sum(picked[0:k], axis=0, keepdims=True)]
    for r1 in range(1, half):
        lo = k + (r1 - 1) * half
        counts.append(jnp.sum(picked[lo:lo + half], axis=0, keepdims=True))
    lo = k + (half - 1) * half
    for r in range(half):
        counts.append(picked[lo + r:lo + r + 1])
    n = jnp.zeros_like(s1)
    for r1 in range(k):
        n = jnp.where(rank1 == float(r1), counts[r1], n)

    ranked = jnp.sum(jnp.where(rank != _NOT_TOP, 1.0, 0.0), axis=0, keepdims=True)
    chosen = ranked[:, :tl] + ranked[:, tl:] + jnp.sum(picked, axis=0, keepdims=True)
    return rank2, jnp.exp(s2 - b[0:1]), n, jnp.exp(s1 - a[0:1]) / z, chosen


def _route_kernel(x_ref, wq_ref, keys_ref, r2_ref, e2_ref, n_ref, c_ref, q_scr, *, hs):
    step = pl.program_id(1)
    ngrp = q_scr.shape[0]
    dk = q_scr.shape[2]

    @pl.when(step == 0)
    def _():
        q = jnp.dot(x_ref[...], wq_ref[...], preferred_element_type=F32)
        for g in range(ngrp):
            q_scr[g] = q[:, g * dk:(g + 1) * dk].astype(q_scr.dtype)

    tl = x_ref.shape[0]
    s1 = jnp.concatenate([lax.dot_general(keys_ref[2 * (step * hs + hh)], q_scr[2 * (step * hs + hh)], _NT,
                                          preferred_element_type=F32) for hh in range(hs)], axis=1)
    s2 = jnp.concatenate([lax.dot_general(keys_ref[2 * (step * hs + hh) + 1], q_scr[2 * (step * hs + hh) + 1],
                                          _NT, preferred_element_type=F32) for hh in range(hs)], axis=1)

    def write(rank2, e2, n, c):
        for hh in range(hs):
            sl = slice(hh * tl, (hh + 1) * tl)
            r2_ref[hh] = rank2[:, sl].astype(r2_ref.dtype)
            e2_ref[hh] = e2[:, sl].astype(e2_ref.dtype)
            n_ref[hh] = n[:, sl]
            c_ref[hh] = c[:, sl]

    fast = _route_head(s1, s2, exact=False)
    no_ties = jnp.max(fast[4]) == 3.0 * PEER_TOPK
    lax.cond(no_ties, lambda: write(*fast[:4]), lambda: write(*_route_head(s1, s2, exact=True)[:4]))


def _route(x1b, wq, keys):
    t, d = x1b.shape
    nq = wq.shape[1]
    ngrp = 2 * PEER_HEADS
    assert nq == ngrp * PEER_DHALF and PEER_TOPK == 16 and PEER_NKEYS % 8 == 0
    tl = _tile(t, 256)
    hs = min(2, PEER_HEADS)
    shp = jax.ShapeDtypeStruct((PEER_HEADS, PEER_NKEYS, t), F32)
    shp_b = jax.ShapeDtypeStruct((PEER_HEADS, PEER_NKEYS, t), BF16)
    ospec = pl.BlockSpec((hs, PEER_NKEYS, tl), lambda i, h: (h, 0, i))
    vmem = (2 * tl * d * 2 + d * nq * 2 + 2 * ngrp * PEER_NKEYS * PEER_DHALF * 2
            + 8 * PEER_NKEYS * tl * 4 + ngrp * tl * PEER_DHALF * 2 + tl * nq * 4 + (16 << 20))
    return pl.pallas_call(
        functools.partial(_route_kernel, hs=hs),
        out_shape=(shp_b, shp_b, shp, shp),
        grid=(t // tl, PEER_HEADS // hs),
        in_specs=[
            pl.BlockSpec((tl, d), lambda i, h: (i, 0)),
            pl.BlockSpec((d, nq), lambda i, h: (0, 0), pipeline_mode=pl.Buffered(1)),
            pl.BlockSpec((ngrp, PEER_NKEYS, PEER_DHALF), lambda i, h: (0, 0, 0)),
        ],
        out_specs=(ospec, ospec, ospec, ospec),
        scratch_shapes=[pltpu.VMEM((ngrp, tl, PEER_DHALF), BF16)],
        compiler_params=_params(("parallel", "arbitrary"), vmem),
        name="route",
    )(x1b, wq, keys)


def _vt_kernel(v_ref, o_ref):
    o_ref[...] = v_ref[...].T.astype(o_ref.dtype)


def _transpose_to_bf16(v):
    ne, d = v.shape
    te = _tile(ne, 256)
    vmem = 2 * (te * d * 4 + d * te * 2) + 3 * te * d * 4 + (2 << 20)
    return pl.pallas_call(
        _vt_kernel,
        out_shape=jax.ShapeDtypeStruct((d, ne), BF16),
        grid=(ne // te,),
        in_specs=[pl.BlockSpec((te, d), lambda i: (i, 0))],
        out_specs=pl.BlockSpec((d, te), lambda i: (0, i)),
        compiler_params=_params(("parallel",), vmem),
        name="vtranspose",
    )(v)


def _peer_kernel(x_ref, u_ref, vt_ref, r2_ref, e2_ref, n_ref, c_ref, yt_ref, g_ref, *, rows):
    j = pl.program_id(1)
    nk = PEER_NKEYS

    @pl.when(j == 0)
    def _():
        yt_ref[...] = jnp.zeros_like(yt_ref)

    for a in range(rows):
        i1 = j * rows + a
        g = None
        for h in range(PEER_HEADS):
            nrow = n_ref[h, pl.ds(i1, 1), :].astype(BF16)
            crow = c_ref[h, pl.ds(i1, 1), :].astype(BF16)
            term = jnp.where(r2_ref[h] < nrow, e2_ref[h] * crow, jnp.zeros((), BF16))
            g = term if g is None else g + term
        g_ref[a * nk:(a + 1) * nk, :] = g

    te = rows * nk
    hte = te // PEER_SPLIT
    x = x_ref[...]
    acts = []
    for s in range(PEER_SPLIT):
        sl = slice(s * hte, (s + 1) * hte)
        ht = lax.dot_general(u_ref[sl, :], x, _NT, preferred_element_type=F32)
        gelu = ht * (lax.erf(ht * (2.0 ** -0.5)) + 1.0) * 0.5
        acts.append(gelu.astype(BF16) * g_ref[sl, :])
    yt_ref[...] += jnp.dot(vt_ref[...], jnp.concatenate(acts, axis=0), preferred_element_type=F32)


def _peer_dense(x1b, u, vt, r2, e2, nn, cc):
    t, d = x1b.shape
    ne = u.shape[0]
    assert ne == PEER_NKEYS * PEER_NKEYS and vt.shape == (d, ne)
    tm = _tile(t, 512)
    rows = 4
    te = rows * PEER_NKEYS
    hk = PEER_HEADS * PEER_NKEYS
    bspec = pl.BlockSpec((PEER_HEADS, PEER_NKEYS, tm), lambda i, j: (0, 0, i),
                         pipeline_mode=pl.Buffered(1))
    vmem = (tm * d * 2 + 4 * te * d * 2 + 2 * hk * tm * 2 + 2 * hk * tm * 4 + 2 * d * tm * 4
            + te * tm * 2 + 3 * te * tm * 4 + (4 << 20))
    return pl.pallas_call(
        functools.partial(_peer_kernel, rows=rows),
        out_shape=jax.ShapeDtypeStruct((d, t), F32),
        grid=(t // tm, ne // te),
        in_specs=[
            pl.BlockSpec((tm, d), lambda i, j: (i, 0), pipeline_mode=pl.Buffered(1)),
            pl.BlockSpec((te, d), lambda i, j: (j, 0)),
            pl.BlockSpec((d, te), lambda i, j: (0, j)),
            bspec, bspec, bspec, bspec,
        ],
        out_specs=pl.BlockSpec((d, tm), lambda i, j: (0, i)),
        scratch_shapes=[pltpu.VMEM((te, tm), BF16)],
        compiler_params=_params(("parallel", "arbitrary"), vmem),
        name="peer",
    )(x1b, u, vt, r2, e2, nn, cc)


def _ln2_kernel(x_ref, yt_ref, w_ref, b_ref, o_ref):
    z = DEEPNORM_ALPHA * x_ref[...] + yt_ref[...].T
    mu = jnp.mean(z, axis=-1, keepdims=True)
    d = z - mu
    var = jnp.mean(d * d, axis=-1, keepdims=True)
    o_ref[...] = (d * lax.rsqrt(var + NORM_EPS) * w_ref[...] + b_ref[...]).astype(o_ref.dtype)


def _ln2(x1f, yt, ln_w, ln_b):
    t, d = x1f.shape
    tm = _tile(t, 256)
    vmem = 2 * 3 * tm * d * 4 + 3 * tm * d * 4 + (4 << 20)
    return pl.pallas_call(
        _ln2_kernel,
        out_shape=jax.ShapeDtypeStruct((t, d), F32),
        grid=(t // tm,),
        in_specs=[
            pl.BlockSpec((tm, d), lambda i: (i, 0)),
            pl.BlockSpec((d, tm), lambda i: (0, i)),
            pl.BlockSpec((1, d), lambda i: (0, 0)),
            pl.BlockSpec((1, d), lambda i: (0, 0)),
        ],
        out_specs=pl.BlockSpec((tm, d), lambda i: (i, 0)),
        compiler_params=_params(("parallel",), vmem),
        name="ln2",
    )(x1f, yt, ln_w.astype(F32).reshape(1, d), ln_b.astype(F32).reshape(1, d))


def kernel(x, w_in, da_lambda, da_subln_w, w_da_out, ret_gn_w, ret_gn_b, w_ret_out, w_out,
           ln1_w, ln1_b, peer_w_query, peer_sub_keys, peer_u, peer_v, ln2_w, ln2_b):
    bsz, seq, d = x.shape
    t = bsz * seq
    da_qk_w, da_v_w = DA_HEADS * 2 * DA_DH, DA_HEADS * DA_DV
    ret_qk_w, ret_v_w = RET_HEADS * RET_DK, RET_HEADS * RET_DV
    splits = (da_qk_w, da_qk_w, da_v_w, ret_qk_w, ret_qk_w, ret_v_w, ret_v_w, d, d)
    offs = [0]
    for s in splits:
        offs.append(offs[-1] + s)
    o_q, o_k, o_v, o_rq, o_rk, o_rv, o_rg, o_ga, o_gr = offs[:-1]

    xf = x.reshape(t, d)
    for l in range(DEPTH):
        xb = xf.astype(BF16)
        proj = _matmul(xb, w_in[l].astype(BF16), BF16, "proj")
        a = _diff_attention(proj, da_lambda[l], da_subln_w[l], bsz, seq, o_q, o_k, o_v, l)
        r = _retention(proj, ret_gn_w[l], ret_gn_b[l], bsz, seq, o_rq, o_rk, o_rv, o_rg)
        merged = _merge(a, w_da_out[l].astype(BF16), r, w_ret_out[l].astype(BF16), proj, o_ga, o_gr, d)
        x1f, x1b = _out_ln(merged, w_out[l].astype(BF16), xf, ln1_w[l], ln1_b[l])
        keys = peer_sub_keys[l].astype(BF16).reshape(2 * PEER_HEADS, PEER_NKEYS, PEER_DHALF)
        r2, e2, nn, cc = _route(x1b, peer_w_query[l].astype(BF16), keys)
        yt = _peer_dense(x1b, peer_u[l].astype(BF16), _transpose_to_bf16(peer_v[l]), r2, e2, nn, cc)
        xf = _ln2(x1f, yt, ln2_w[l], ln2_b[l])
    return xf.reshape(bsz, seq, d)
```
